```python
import jax, jax.numpy as jnp
from jax import lax
import numpy as np

D_MODEL = 1024
BATCH = 16
SEQ = 2048
DEPTH = 1

CHUNK = 64
GMLP_BLOCK = 128
GMLP_WIDTH = D_MODEL
GMLP_GROUPS = 8
GMLP_GROUP_DIM = GMLP_WIDTH // GMLP_GROUPS
MLSTM_HEADS = 4
MLSTM_HEAD_DIM = 256
MLSTM_WIDTH = MLSTM_HEADS * MLSTM_HEAD_DIM
CONV_K = 4
FFN_DIM = 4 * D_MODEL
EPS = 1e-6

OFF_U = 0
OFF_V = OFF_U + GMLP_WIDTH
OFF_Q = OFF_V + GMLP_WIDTH
OFF_K = OFF_Q + MLSTM_WIDTH
OFF_MV = OFF_K + MLSTM_WIDTH
OFF_O = OFF_MV + MLSTM_WIDTH
OFF_I = OFF_O + MLSTM_WIDTH
OFF_F = OFF_I + MLSTM_HEADS
OFF_GA = OFF_F + MLSTM_HEADS
OFF_GB = OFF_GA + D_MODEL
IN_COLS = OFF_GB + D_MODEL

kernel_name = "hybrid_gmlp_mlstm_adaln_block"


def rmsnorm(x, g):
    xf = x.astype(jnp.float32)
    y = xf * lax.rsqrt(jnp.mean(xf * xf, axis=-1, keepdims=True) + EPS)
    return y.astype(x.dtype) * g


def layernorm(x, g, b):
    xf = x.astype(jnp.float32)
    mu = jnp.mean(xf, axis=-1, keepdims=True)
    var = jnp.mean(jnp.square(xf - mu), axis=-1, keepdims=True)
    y = (xf - mu) * lax.rsqrt(var + EPS)
    return y.astype(x.dtype) * g + b


def modulate(xn, shift, scale):
    return xn * (1.0 + scale[:, None, :]) + shift[:, None, :]


def causal_dwconv(x, w, b):
    C = x.shape[-1]
    y = lax.conv_general_dilated(x, w[:, None, :], window_strides=(1,),
                                 padding=[(CONV_K - 1, 0)],
                                 dimension_numbers=('NWC', 'WIO', 'NWC'),
                                 feature_group_count=C)
    return y + b


def gmlp_branch(z_u, z_v, ln_g, ln_b, w_s, b_s):
    u = jax.nn.gelu(z_u)
    v = layernorm(jax.nn.gelu(z_v), ln_g, ln_b)
    B, S, _ = v.shape
    nb = S // GMLP_BLOCK
    v = v.reshape(B, nb, GMLP_BLOCK, GMLP_GROUPS, GMLP_GROUP_DIM)
    cid = jnp.arange(GMLP_BLOCK) // CHUNK
    mask = cid[None, :] <= cid[:, None]
    ws = jnp.where(mask[None], w_s, jnp.zeros_like(w_s))
    mixed = jnp.einsum('gts,bnsgc->bntgc', ws, v) + b_s.T[None, None, :, :, None]
    return u * mixed.reshape(B, S, GMLP_WIDTH)


def mlstm_branch(q, k, v, o_pre, i_pre, f_pre, hn_g):
    out_dtype = q.dtype
    B, S, _ = q.shape
    H, dh, L = MLSTM_HEADS, MLSTM_HEAD_DIM, CHUNK
    nc = S // L

    def heads(t):
        return t.astype(jnp.float32).reshape(B, nc, L, H, dh).transpose(1, 0, 3, 2, 4)

    def gates(t):
        return t.astype(jnp.float32).reshape(B, nc, L, H).transpose(1, 0, 3, 2)

    qh = heads(q) * (dh ** -0.5)
    kh = heads(k)
    vh = heads(v)
    li = gates(i_pre)
    lf = jax.nn.log_sigmoid(gates(f_pre))
    causal = jnp.tril(jnp.ones((L, L), dtype=bool))

    def step(carry, xs):
        C, n, m = carry
        qc, kc, vc, lic, lfc = xs
        bcum = jnp.cumsum(lfc, axis=-1)
        dmat = bcum[..., :, None] - bcum[..., None, :] + lic[..., None, :]
        dmat = jnp.where(causal, dmat, -jnp.inf)
        m_inter = bcum + m[..., None]
        m_t = jnp.maximum(m_inter, jnp.max(dmat, axis=-1))
        wts = jnp.exp(dmat - m_t[..., None])
        s = jnp.einsum('bhtd,bhsd->bhts', qc, kc) * wts
        inter = jnp.exp(m_inter - m_t)
        num = (jnp.einsum('bhts,bhsv->bhtv', s, vc)
               + inter[..., None] * jnp.einsum('bhtk,bhkv->bhtv', qc, C))
        den = jnp.sum(s, axis=-1) + inter * jnp.einsum('bhtk,bhk->bht', qc, n)
        h = num / jnp.maximum(jnp.abs(den), jnp.exp(-m_t))[..., None]
        b_last = bcum[..., -1]
        g = b_last[..., None] - bcum + lic
        m_new = jnp.maximum(b_last + m, jnp.max(g, axis=-1))
        decay = jnp.exp(b_last + m - m_new)
        wk = jnp.exp(g - m_new[..., None])[..., None] * kc
        C = decay[..., None, None] * C + jnp.einsum('bhsk,bhsv->bhkv', wk, vc)
        n = decay[..., None] * n + jnp.sum(wk, axis=2)
        return (C, n, m_new), h

    init = (jnp.zeros((B, H, dh, dh), jnp.float32),
            jnp.zeros((B, H, dh), jnp.float32),
            jnp.zeros((B, H), jnp.float32))
    _, h = lax.scan(step, init, (qh, kh, vh, li, lf))
    h = h.transpose(1, 0, 3, 2, 4).reshape(B, S, H, dh)
    mu = jnp.mean(h, axis=-1, keepdims=True)
    var = jnp.mean(jnp.square(h - mu), axis=-1, keepdims=True)
    h = (h - mu) * lax.rsqrt(var + EPS) * hn_g.astype(jnp.float32).reshape(H, dh)
    h = h.reshape(B, S, MLSTM_WIDTH) * jax.nn.sigmoid(o_pre.astype(jnp.float32))
    return h.astype(out_dtype)


def setup_inputs(seed: int = 0) -> dict:
    key = jax.random.key(seed)
    ks = jax.random.split(key, 24)
    f32 = jnp.float32
    D = D_MODEL

    def nrm(k, shape, scale):
        return jax.random.normal(k, shape, f32) * scale

    f_bias = jnp.broadcast_to(jnp.linspace(3.0, 6.0, MLSTM_HEADS, dtype=f32), (DEPTH, MLSTM_HEADS))
    gate_b = jnp.stack([nrm(ks[7], (DEPTH, MLSTM_HEADS), 0.1),
                        f_bias + nrm(ks[8], (DEPTH, MLSTM_HEADS), 0.1)], axis=1)
    return {
        "x": nrm(ks[0], (BATCH, SEQ, D), 1.0),
        "c": nrm(ks[1], (BATCH, D), 1.0),
        "w_ada": nrm(ks[2], (DEPTH, D, 6 * D), 0.5 * D ** -0.5),
        "b_ada": nrm(ks[3], (DEPTH, 6 * D), 0.02),
        "norm1_g": 1.0 + nrm(ks[4], (DEPTH, D), 0.02),
        "w_in": nrm(ks[5], (DEPTH, D, IN_COLS), D ** -0.5),
        "conv_w": nrm(ks[6], (DEPTH, CONV_K, 2 * MLSTM_WIDTH), CONV_K ** -0.5),
        "conv_b": nrm(ks[9], (DEPTH, 2 * MLSTM_WIDTH), 0.02),
        "mlstm_gate_b": gate_b,
        "gmlp_ln_g": 1.0 + nrm(ks[10], (DEPTH, GMLP_WIDTH), 0.02),
        "gmlp_ln_b": nrm(ks[11], (DEPTH, GMLP_WIDTH), 0.02),
        "gmlp_ws": nrm(ks[12], (DEPTH, GMLP_GROUPS, GMLP_BLOCK, GMLP_BLOCK), 0.5 * GMLP_BLOCK ** -0.5),
        "gmlp_bs": 1.0 + nrm(ks[13], (DEPTH, GMLP_GROUPS, GMLP_BLOCK), 0.1),
        "mlstm_hn_g": 1.0 + nrm(ks[14], (DEPTH, MLSTM_WIDTH), 0.02),
        "w_out": nrm(ks[15], (DEPTH, D, D), D ** -0.5),
        "norm2_g": 1.0 + nrm(ks[16], (DEPTH, D), 0.02),
        "w_ff1": nrm(ks[17], (DEPTH, D, FFN_DIM), D ** -0.5),
        "w_ff2": nrm(ks[18], (DEPTH, FFN_DIM, D), FFN_DIM ** -0.5),
        "final_g": 1.0 + nrm(ks[19], (D,), 0.02),
    }


def reference(x, c, w_ada, b_ada, norm1_g, w_in, conv_w, conv_b, mlstm_gate_b,
              gmlp_ln_g, gmlp_ln_b, gmlp_ws, gmlp_bs, mlstm_hn_g, w_out,
              norm2_g, w_ff1, w_ff2, final_g):
    h = x
    c_act = jax.nn.silu(c)
    for l in range(DEPTH):
        mod = c_act @ w_ada[l] + b_ada[l]
        sh1, sc1, g1, sh2, sc2, g2 = jnp.split(mod, 6, axis=-1)

        xn = modulate(rmsnorm(h, norm1_g[l]), sh1, sc1)
        p = xn @ w_in[l]
        qk = jax.nn.silu(causal_dwconv(p[..., OFF_Q:OFF_MV], conv_w[l], conv_b[l]))
        q, k = jnp.split(qk, 2, axis=-1)
        y_a = gmlp_branch(p[..., OFF_U:OFF_V], p[..., OFF_V:OFF_Q],
                          gmlp_ln_g[l], gmlp_ln_b[l], gmlp_ws[l], gmlp_bs[l])
        y_b = mlstm_branch(q, k, p[..., OFF_MV:OFF_O], p[..., OFF_O:OFF_I],
                           p[..., OFF_I:OFF_F] + mlstm_gate_b[l, 0],
                           p[..., OFF_F:OFF_GA] + mlstm_gate_b[l, 1],
                           mlstm_hn_g[l])
        merged = (jax.nn.sigmoid(p[..., OFF_GA:OFF_GB]) * y_a
                  + jax.nn.sigmoid(p[..., OFF_GB:IN_COLS]) * y_b)
        h = h + g1[:, None, :] * (merged @ w_out[l])

        xn = modulate(rmsnorm(h, norm2_g[l]), sh2, sc2)
        ff = jnp.square(jax.nn.relu(xn @ w_ff1[l])) @ w_ff2[l]
        h = h + g2[:, None, :] * ff
    return rmsnorm(h, final_g)
```

```python
import functools

import jax
import jax.numpy as jnp
from jax import lax
from jax.experimental import pallas as pl
from jax.experimental.pallas import tpu as pltpu

D = 1024
BATCH = 16
SEQ = 2048
TOKENS = BATCH * SEQ
GMLP_BLOCK = 128
GMLP_GROUPS = 8
GROUP_DIM = D // GMLP_GROUPS
HEADS = 4
HEAD_DIM = 256
CONV_K = 4
FFN = 4 * D
EPS = 1e-6
LANES = 128
SUBLANES = 8

OFF_U, OFF_V, OFF_Q, OFF_K, OFF_MV, OFF_O = 0, D, 2 * D, 3 * D, 4 * D, 5 * D
OFF_I = 6 * D
OFF_F = OFF_I + HEADS
OFF_GA = OFF_F + HEADS
OFF_GB = OFF_GA + D
N_SEG = 8

TM_IN = 256
L_MIX = 256
TM_FFN = 512
FFN_CHUNK = 1024
ADA_COLS = 1536

F32 = jnp.float32
BF16 = jnp.bfloat16
VMEM_LIMIT = 56 * 1024 * 1024


def _resident(shape):
    nd = len(shape)
    return pl.BlockSpec(shape, lambda *_: (0,) * nd, pipeline_mode=pl.Buffered(1))


def _ada_kernel(c_ref, w_ref, b_ref, o_ref):
    c = c_ref[...]
    ca = c * jax.nn.sigmoid(c)
    o_ref[...] = jnp.dot(ca, w_ref[...], preferred_element_type=F32,
                         precision=lax.Precision.HIGHEST) + b_ref[...]


def _ada(c, w, b):
    n = w.shape[1]
    return pl.pallas_call(
        _ada_kernel,
        grid=(n // ADA_COLS,),
        in_specs=[pl.BlockSpec((BATCH, D), lambda j: (0, 0)),
                  pl.BlockSpec((D, ADA_COLS), lambda j: (0, j)),
                  pl.BlockSpec((1, ADA_COLS), lambda j: (0, j))],
        out_specs=pl.BlockSpec((BATCH, ADA_COLS), lambda j: (0, j)),
        out_shape=jax.ShapeDtypeStruct((BATCH, n), F32),
        compiler_params=pltpu.CompilerParams(vmem_limit_bytes=VMEM_LIMIT),
        name="ada",
    )(c, w, b.reshape(1, n))


def _rms_mod(x, g, shift, scale):
    y = x * lax.rsqrt(jnp.mean(x * x, axis=-1, keepdims=True) + EPS)
    return (y * g) * (1.0 + scale) + shift


def _conv_silu(z, buf_ref, w, b, first):
    tm = z.shape[0]

    @pl.when(first)
    def _():
        buf_ref[0:SUBLANES, :] = jnp.zeros((SUBLANES, z.shape[1]), F32)

    buf_ref[SUBLANES:SUBLANES + tm, :] = z
    acc = b + w[CONV_K - 1:CONV_K, :] * z
    for j in range(CONV_K - 1):
        off = SUBLANES - (CONV_K - 1) + j
        acc = acc + w[j:j + 1, :] * buf_ref[off:off + tm, :]
    buf_ref[0:SUBLANES, :] = buf_ref[tm:tm + SUBLANES, :]
    return acc * jax.nn.sigmoid(acc)


def _inproj_kernel(x_ref, sh_ref, sc_ref, g_ref, w_ref, wg_ref, gb_ref, cw_ref, cb_ref,
                   lng_ref, lnb_ref,
                   u_ref, v_ref, q_ref, k_ref, mv_ref, o_ref, ga_ref, gbo_ref, gate_ref,
                   qbuf, kbuf):
    i = pl.program_id(0)
    first = (i % (SEQ // TM_IN)) == 0
    xn = _rms_mod(x_ref[...], g_ref[...], sh_ref[...], sc_ref[...]).astype(BF16)

    def seg(k):
        return jnp.dot(xn, w_ref[:, k * D:(k + 1) * D], preferred_element_type=F32)

    u_ref[...] = jax.nn.gelu(seg(0)).astype(BF16)

    gv = jax.nn.gelu(seg(1))
    mu = jnp.mean(gv, axis=-1, keepdims=True)
    dv = gv - mu
    var = jnp.mean(dv * dv, axis=-1, keepdims=True)
    v_ref[...] = ((dv * lax.rsqrt(var + EPS)) * lng_ref[...] + lnb_ref[...]).astype(BF16)

    q = _conv_silu(seg(2), qbuf, cw_ref[:, 0:D], cb_ref[:, 0:D], first)
    q_ref[...] = (q * (HEAD_DIM ** -0.5)).astype(BF16)
    k = _conv_silu(seg(3), kbuf, cw_ref[:, D:2 * D], cb_ref[:, D:2 * D], first)
    k_ref[...] = k.astype(BF16)

    mv_ref[...] = seg(4).astype(BF16)
    o_ref[...] = jax.nn.sigmoid(seg(5)).astype(BF16)
    ga_ref[...] = jax.nn.sigmoid(seg(6)).astype(BF16)
    gbo_ref[...] = jax.nn.sigmoid(seg(7)).astype(BF16)

    gp = jnp.dot(xn, wg_ref[...], preferred_element_type=F32) + gb_ref[...]
    lane = lax.broadcasted_iota(jnp.int32, gp.shape, 1)
    gate_ref[...] = jnp.where(lane < HEADS, gp, jax.nn.log_sigmoid(gp))


def _inproj(x2, sh1, sc1, norm_g, w_all, w_gate, gate_bias, conv_w, conv_b, ln_g, ln_b):
    nsb = SEQ // TM_IN
    row = lambda i: (i, 0)
    per_batch = lambda i: (i // nsb, 0, 0)
    seg_out = pl.BlockSpec((TM_IN, D), row)
    seg_shape = jax.ShapeDtypeStruct((TOKENS, D), BF16)
    return pl.pallas_call(
        _inproj_kernel,
        grid=(TOKENS // TM_IN,),
        in_specs=[pl.BlockSpec((TM_IN, D), row),
                  pl.BlockSpec((None, 1, D), per_batch),
                  pl.BlockSpec((None, 1, D), per_batch),
                  _resident((1, D)),
                  _resident((D, N_SEG * D)),
                  _resident((D, LANES)),
                  _resident((1, LANES)),
                  _resident((CONV_K, 2 * D)),
                  _resident((1, 2 * D)),
                  _resident((1, D)),
                  _resident((1, D))],
        out_specs=[seg_out] * N_SEG + [pl.BlockSpec((TM_IN, LANES), row)],
        out_shape=[seg_shape] * N_SEG + [jax.ShapeDtypeStruct((TOKENS, LANES), F32)],
        scratch_shapes=[pltpu.VMEM((TM_IN + SUBLANES, D), F32),
                        pltpu.VMEM((TM_IN + SUBLANES, D), F32)],
        compiler_params=pltpu.CompilerParams(dimension_semantics=("arbitrary",),
                                             vmem_limit_bytes=VMEM_LIMIT),
        name="inproj",
    )(x2, sh1, sc1, norm_g, w_all, w_gate, gate_bias, conv_w, conv_b, ln_g, ln_b)


def _cumsum_rows(x):
    n = x.shape[0]
    row = lax.broadcasted_iota(jnp.int32, x.shape, 0)
    k = 1
    while k < n:
        x = x + jnp.where(row >= k, pltpu.roll(x, k, axis=0), 0.0)
        k *= 2
    return x


def _mixer_kernel(x_ref, g1_ref, u_ref, v_ref, q_ref, k_ref, mv_ref, o_ref, ga_ref, gb_ref,
                  gate_ref, ws_ref, bst_ref, hng_ref, wout_ref,
                  h1_ref,
                  c_scr, n_scr, m_scr, merged_scr):
    L = L_MIX
    j = pl.program_id(1)

    @pl.when(j == 0)
    def _():
        c_scr[...] = jnp.zeros(c_scr.shape, F32)
        n_scr[...] = jnp.zeros(n_scr.shape, F32)
        m_scr[...] = jnp.zeros(m_scr.shape, F32)

    gates = gate_ref[...]
    bcum = _cumsum_rows(gates)
    gates_t = gates.T
    bcum_t = bcum.T
    row = lax.broadcasted_iota(jnp.int32, (L, L), 0)
    col = lax.broadcasted_iota(jnp.int32, (L, L), 1)
    causal = col <= row

    brow = lax.broadcasted_iota(jnp.int32, (GMLP_BLOCK, GMLP_BLOCK), 0) // 64
    bcol = lax.broadcasted_iota(jnp.int32, (GMLP_BLOCK, GMLP_BLOCK), 1) // 64
    wmask = bcol <= brow
    nblk = L // GMLP_BLOCK

    for h in range(HEADS):
        hs = slice(h * HEAD_DIM, (h + 1) * HEAD_DIM)
        li_c = gates[:, h:h + 1]
        li_r = gates_t[h:h + 1, :]
        b_c = bcum[:, HEADS + h:HEADS + h + 1]
        b_r = bcum_t[HEADS + h:HEADS + h + 1, :]
        m_prev = m_scr[h, 0:1, 0:1]
        dmat = jnp.where(causal, (b_c - b_r) + li_r, -jnp.inf)
        m_inter = b_c + m_prev
        m_t = jnp.maximum(m_inter, jnp.max(dmat, axis=-1, keepdims=True))
        wts = jnp.exp(dmat - m_t)
        qh = q_ref[:, hs]
        kh = k_ref[:, hs]
        vh = mv_ref[:, hs]
        s = lax.dot_general(qh, kh, (((1,), (1,)), ((), ())), preferred_element_type=F32) * wts
        inter = jnp.exp(m_inter - m_t)
        c_old = c_scr[h]
        n_old = n_scr[h, 0:1, :]
        num = (jnp.dot(s.astype(BF16), vh, preferred_element_type=F32)
               + inter * jnp.dot(qh, c_old.astype(BF16), preferred_element_type=F32))
        den = (jnp.sum(s, axis=-1, keepdims=True)
               + inter * jnp.sum(qh.astype(F32) * n_old, axis=-1, keepdims=True))
        hh = num / jnp.maximum(jnp.abs(den), jnp.exp(-m_t))

        b_last = b_c[L - 1:L, :]
        g_c = (b_last - b_c) + li_c
        m_new = jnp.maximum(b_last + m_prev, jnp.max(g_c, axis=0, keepdims=True))
        decay = jnp.exp((b_last + m_prev) - m_new)
        wk = jnp.exp(g_c - m_new) * kh.astype(F32)
        c_scr[h] = decay * c_old + jnp.dot(wk.T.astype(BF16), vh, preferred_element_type=F32)
        n_scr[h] = jnp.broadcast_to(decay * n_old + jnp.sum(wk, axis=0, keepdims=True),
                                    (SUBLANES, HEAD_DIM))
        m_scr[h] = jnp.broadcast_to(m_new, (SUBLANES, LANES))

        mu = jnp.mean(hh, axis=-1, keepdims=True)
        dh = hh - mu
        var = jnp.mean(dh * dh, axis=-1, keepdims=True)
        y_b = (dh * lax.rsqrt(var + EPS)) * hng_ref[:, hs] * o_ref[:, hs].astype(F32)

        for g in range(2 * h, 2 * h + 2):
            gs = slice(g * GROUP_DIM, (g + 1) * GROUP_DIM)
            wsg = jnp.where(wmask, ws_ref[g], 0.0).astype(BF16)
            vcat = jnp.concatenate(
                [v_ref[n * GMLP_BLOCK:(n + 1) * GMLP_BLOCK, gs] for n in range(nblk)], axis=1)
            mixed = jnp.dot(wsg, vcat, preferred_element_type=F32) + bst_ref[:, g:g + 1]
            yb_g = y_b[:, (g - 2 * h) * GROUP_DIM:(g - 2 * h + 1) * GROUP_DIM]
            for n in range(nblk):
                rs = slice(n * GMLP_BLOCK, (n + 1) * GMLP_BLOCK)
                y_a = u_ref[rs, gs].astype(F32) * mixed[:, n * GMLP_BLOCK:(n + 1) * GMLP_BLOCK]
                merged = (ga_ref[rs, gs].astype(F32) * y_a
                          + gb_ref[rs, gs].astype(F32) * yb_g[rs, :])
                merged_scr[rs, gs] = merged.astype(BF16)

    proj = jnp.dot(merged_scr[...], wout_ref[...], preferred_element_type=F32)
    h1_ref[...] = x_ref[...] + g1_ref[...] * proj


def _mixer(x2, g1, u, v, q, k, mv, o, ga, gb, gates, ws, bs_t, hn_g, w_out):
    nsb = SEQ // L_MIX
    row = lambda b, j: (b * nsb + j, 0)
    seg_in = pl.BlockSpec((L_MIX, D), row)
    return pl.pallas_call(
        _mixer_kernel,
        grid=(BATCH, nsb),
        in_specs=[seg_in,
                  pl.BlockSpec((None, 1, D), lambda b, j: (b, 0, 0)),
                  seg_in, seg_in, seg_in, seg_in, seg_in, seg_in, seg_in, seg_in,
                  pl.BlockSpec((L_MIX, LANES), row),
                  _resident((GMLP_GROUPS, GMLP_BLOCK, GMLP_BLOCK)),
                  _resident((GMLP_BLOCK, GMLP_GROUPS)),
                  _resident((1, D)),
                  _resident((D, D))],
        out_specs=pl.BlockSpec((L_MIX, D), row),
        out_shape=jax.ShapeDtypeStruct((TOKENS, D), F32),
        scratch_shapes=[pltpu.VMEM((HEADS, HEAD_DIM, HEAD_DIM), F32),
                        pltpu.VMEM((HEADS, SUBLANES, HEAD_DIM), F32),
                        pltpu.VMEM((HEADS, SUBLANES, LANES), F32),
                        pltpu.VMEM((L_MIX, D), BF16)],
        compiler_params=pltpu.CompilerParams(dimension_semantics=("arbitrary", "arbitrary"),
                                             vmem_limit_bytes=VMEM_LIMIT),
        name="mixer",
    )(x2, g1, u, v, q, k, mv, o, ga, gb, gates, ws, bs_t, hn_g, w_out)


def _ffn_kernel(h_ref, sh_ref, sc_ref, g2_ref, ng_ref, w1_ref, w2_ref, fg_ref, o_ref):
    h = h_ref[...]
    xn = _rms_mod(h, ng_ref[...], sh_ref[...], sc_ref[...]).astype(BF16)
    acc = jnp.zeros((TM_FFN, D), F32)
    for c in range(FFN // FFN_CHUNK):
        cs = slice(c * FFN_CHUNK, (c + 1) * FFN_CHUNK)
        a = jnp.maximum(jnp.dot(xn, w1_ref[:, cs], preferred_element_type=F32), 0.0)
        acc = acc + jnp.dot((a * a).astype(BF16), w2_ref[cs, :], preferred_element_type=F32)
    h2 = h + g2_ref[...] * acc
    y = h2 * lax.rsqrt(jnp.mean(h2 * h2, axis=-1, keepdims=True) + EPS)
    o_ref[...] = y * fg_ref[...]


def _ffn(h1, sh2, sc2, g2, norm_g, w1, w2, final_g):
    nsb = SEQ // TM_FFN
    row = lambda i: (i, 0)
    per_batch = lambda i: (i // nsb, 0, 0)
    return pl.pallas_call(
        _ffn_kernel,
        grid=(TOKENS // TM_FFN,),
        in_specs=[pl.BlockSpec((TM_FFN, D), row),
                  pl.BlockSpec((None, 1, D), per_batch),
                  pl.BlockSpec((None, 1, D), per_batch),
                  pl.BlockSpec((None, 1, D), per_batch),
                  _resident((1, D)),
                  _resident((D, FFN)),
                  _resident((FFN, D)),
                  _resident((1, D))],
        out_specs=pl.BlockSpec((TM_FFN, D), row),
        out_shape=jax.ShapeDtypeStruct((TOKENS, D), F32),
        compiler_params=pltpu.CompilerParams(dimension_semantics=("arbitrary",),
                                             vmem_limit_bytes=VMEM_LIMIT),
        name="ffn",
    )(h1, sh2, sc2, g2, norm_g, w1, w2, final_g)


def kernel(x, c, w_ada, b_ada, norm1_g, w_in, conv_w, conv_b, mlstm_gate_b, gmlp_ln_g, gmlp_ln_b,
           gmlp_ws, gmlp_bs, mlstm_hn_g, w_out, norm2_g, w_ff1, w_ff2, final_g):
    l = 0
    mod = _ada(c, w_ada[l], b_ada[l]).reshape(BATCH, 6, 1, D)
    sh1, sc1, g1, sh2, sc2, g2 = (mod[:, t] for t in range(6))

    w = w_in[l]
    w_all = jnp.concatenate([w[:, :OFF_I], w[:, OFF_GA:]], axis=1).astype(BF16)
    w_gate = jnp.pad(w[:, OFF_I:OFF_GA], ((0, 0), (0, LANES - 2 * HEADS))).astype(BF16)
    gate_bias = jnp.pad(mlstm_gate_b[l].reshape(1, 2 * HEADS), ((0, 0), (0, LANES - 2 * HEADS)))

    x2 = x.reshape(TOKENS, D)
    u, v, q, k, mv, o, ga, gb, gates = _inproj(
        x2, sh1, sc1, norm1_g[l].reshape(1, D), w_all, w_gate, gate_bias,
        conv_w[l], conv_b[l].reshape(1, 2 * D),
        gmlp_ln_g[l].reshape(1, D), gmlp_ln_b[l].reshape(1, D))

    h1 = _mixer(x2, g1, u, v, q, k, mv, o, ga, gb, gates,
                gmlp_ws[l], gmlp_bs[l].T, mlstm_hn_g[l].reshape(1, D), w_out[l].astype(BF16))

    out = _ffn(h1, sh2, sc2, g2, norm2_g[l].reshape(1, D),
               w_ff1[l].astype(BF16), w_ff2[l].astype(BF16), final_g.reshape(1, D))
    return out.reshape(BATCH, SEQ, D)
```

```python
import jax
import jax.numpy as jnp
from jax import lax
from jax.experimental import pallas as pl
from jax.experimental.pallas import tpu as pltpu

D = 1024
BATCH = 16
SEQ = 2048
TOKENS = BATCH * SEQ
GMLP_BLOCK = 128
GMLP_GROUPS = 8
GROUP_DIM = D // GMLP_GROUPS
HEADS = 4
HEAD_DIM = 256
CONV_K = 4
FFN = 4 * D
EPS = 1e-6
LANES = 128
SUBLANES = 8

OFF_I = 6 * D
OFF_GA = OFF_I + 2 * HEADS
N_SEG = 8
SEG_U, SEG_V, SEG_Q, SEG_K, SEG_MV, SEG_O, SEG_GA, SEG_GB = range(N_SEG)
CW = 256
NCH = D // CW

TM_IN = 512
ROWS_PIECE = 128
L_MIX = 256
TM_FFN = 512
FFN_CHUNK = 1024
ADA_COLS = 1536

F32 = jnp.float32
BF16 = jnp.bfloat16
VMEM_LIMIT = 56 * 1024 * 1024


def _resident(shape):
    nd = len(shape)
    return pl.BlockSpec(shape, lambda *_: (0,) * nd, pipeline_mode=pl.Buffered(1))


def _ada_kernel(c_ref, w_ref, b_ref, o_ref):
    c = c_ref[...]
    ca = c * jax.nn.sigmoid(c)
    o_ref[...] = jnp.dot(ca, w_ref[...], preferred_element_type=F32,
                         precision=lax.Precision.HIGHEST) + b_ref[...]


def _ada(c, w, b):
    n = w.shape[1]
    return pl.pallas_call(
        _ada_kernel,
        grid=(n // ADA_COLS,),
        in_specs=[pl.BlockSpec((BATCH, D), lambda j: (0, 0)),
                  pl.BlockSpec((D, ADA_COLS), lambda j: (0, j)),
                  pl.BlockSpec((1, ADA_COLS), lambda j: (0, j))],
        out_specs=pl.BlockSpec((BATCH, ADA_COLS), lambda j: (0, j)),
        out_shape=jax.ShapeDtypeStruct((BATCH, n), F32),
        compiler_params=pltpu.CompilerParams(vmem_limit_bytes=VMEM_LIMIT),
        name="ada",
    )(c, w, b.reshape(1, n))


def _rms_mod(x, g, shift, scale):
    y = x * lax.rsqrt(jnp.mean(x * x, axis=-1, keepdims=True) + EPS)
    return (y * g) * (1.0 + scale) + shift


def _conv_silu(z, halo_ref, c, w, b):
    tm = z.shape[0]
    zz = jnp.concatenate([halo_ref[c], z], axis=0)
    acc = b + w[CONV_K - 1:CONV_K, :] * z
    for j in range(CONV_K - 1):
        shifted = pltpu.roll(zz, CONV_K - 1 - j, axis=0)[SUBLANES:, :]
        acc = acc + w[j:j + 1, :] * shifted
    halo_ref[c] = z[tm - SUBLANES:, :]
    return acc * jax.nn.sigmoid(acc)


def _inproj_kernel(x_ref, sh_ref, sc_ref, g_ref, w_ref, wg_ref, gb_ref, cw_ref, cb_ref,
                   lng_ref, lnb_ref,
                   u_ref, v_ref, q_ref, kt_ref, mv_ref, o_ref, ga_ref, gbo_ref, gate_ref,
                   xn_scr, gv_scr, s1_scr, mu_scr, rs_scr, halo_scr):
    i = pl.program_id(0)

    @pl.when((i % (SEQ // TM_IN)) == 0)
    def _():
        halo_scr[...] = jnp.zeros(halo_scr.shape, F32)

    xn_scr[...] = _rms_mod(x_ref[...], g_ref[...], sh_ref[...], sc_ref[...]).astype(BF16)
    s1_scr[...] = jnp.zeros(s1_scr.shape, F32)

    def seg(k, c, rs):
        return jnp.dot(xn_scr[rs, :], w_ref[k, c], preferred_element_type=F32)

    pieces = [slice(p * ROWS_PIECE, (p + 1) * ROWS_PIECE) for p in range(TM_IN // ROWS_PIECE)]

    def loop_a(c, carry):
        for rs in pieces:
            gv = jax.nn.gelu(seg(SEG_V, c, rs))
            gv_scr[c, rs, :] = gv
            s1_scr[rs, :] += gv[:, :LANES] + gv[:, LANES:]
            u_ref[c, rs, :] = jax.nn.gelu(seg(SEG_U, c, rs)).astype(BF16)
            q = _conv_silu(seg(SEG_Q, c, rs), halo_scr.at[0], c, cw_ref[0, c], cb_ref[0, c])
            q_ref[c, rs, :] = (q * (HEAD_DIM ** -0.5)).astype(BF16)
            o_ref[c, rs, :] = jax.nn.sigmoid(seg(SEG_O, c, rs)).astype(BF16)
        return carry

    lax.fori_loop(0, NCH, loop_a, 0)

    mu = jnp.sum(s1_scr[...], axis=-1, keepdims=True) * (1.0 / D)
    ssq = jnp.zeros((TM_IN, LANES), F32)
    for c in range(NCH):
        dv = gv_scr[c] - mu
        dv = dv * dv
        ssq = ssq + (dv[:, :LANES] + dv[:, LANES:])
    var = jnp.sum(ssq, axis=-1, keepdims=True) * (1.0 / D)
    mu_scr[...] = jnp.broadcast_to(mu, (TM_IN, CW))
    rs_scr[...] = jnp.broadcast_to(lax.rsqrt(var + EPS), (TM_IN, CW))

    gp = jnp.dot(xn_scr[...], wg_ref[...], preferred_element_type=F32) + gb_ref[...]
    lane = lax.broadcasted_iota(jnp.int32, gp.shape, 1)
    gate_ref[...] = jnp.where(lane < HEADS, gp, jax.nn.log_sigmoid(gp))

    def loop_b(c, carry):
        for rs in pieces:
            v_ref[c, rs, :] = (((gv_scr[c, rs, :] - mu_scr[rs, :]) * rs_scr[rs, :]) * lng_ref[c]
                               + lnb_ref[c]).astype(BF16)
            mv_ref[c, rs, :] = seg(SEG_MV, c, rs).astype(BF16)
            k = _conv_silu(seg(SEG_K, c, rs), halo_scr.at[1], c, cw_ref[1, c], cb_ref[1, c])
            kt_ref[c, :, rs] = k.T.astype(BF16)
            ga_ref[c, rs, :] = jax.nn.sigmoid(seg(SEG_GA, c, rs)).astype(BF16)
            gbo_ref[c, rs, :] = jax.nn.sigmoid(seg(SEG_GB, c, rs)).astype(BF16)
        return carry

    lax.fori_loop(0, NCH, loop_b, 0)


def _inproj(x2, sh1, sc1, norm_g, w_all, w_gate, gate_bias, conv_w, conv_b, ln_g, ln_b):
    nsb = SEQ // TM_IN
    row = lambda i: (i, 0)
    per_batch = lambda i: (i // nsb, 0, 0)
    seg_out = pl.BlockSpec((NCH, TM_IN, CW), lambda i: (0, i, 0))
    seg_shape = jax.ShapeDtypeStruct((NCH, TOKENS, CW), BF16)
    seg_outs = [seg_out] * N_SEG
    seg_shapes = [seg_shape] * N_SEG
    seg_outs[SEG_K] = pl.BlockSpec((NCH, CW, TM_IN), lambda i: (0, 0, i))
    seg_shapes[SEG_K] = jax.ShapeDtypeStruct((NCH, CW, TOKENS), BF16)
    return pl.pallas_call(
        _inproj_kernel,
        grid=(TOKENS // TM_IN,),
        in_specs=[pl.BlockSpec((TM_IN, D), row),
                  pl.BlockSpec((None, 1, D), per_batch),
                  pl.BlockSpec((None, 1, D), per_batch),
                  _resident((1, D)),
                  _resident((N_SEG, NCH, D, CW)),
                  _resident((D, LANES)),
                  _resident((1, LANES)),
                  _resident((2, NCH, CONV_K, CW)),
                  _resident((2, NCH, 1, CW)),
                  _resident((NCH, 1, CW)),
                  _resident((NCH, 1, CW))],
        out_specs=seg_outs + [pl.BlockSpec((TM_IN, LANES), row)],
        out_shape=seg_shapes + [jax.ShapeDtypeStruct((TOKENS, LANES), F32)],
        scratch_shapes=[pltpu.VMEM((TM_IN, D), BF16),
                        pltpu.VMEM((NCH, TM_IN, CW), F32),
                        pltpu.VMEM((TM_IN, LANES), F32),
                        pltpu.VMEM((TM_IN, CW), F32),
                        pltpu.VMEM((TM_IN, CW), F32),
                        pltpu.VMEM((2, NCH, SUBLANES, CW), F32)],
        compiler_params=pltpu.CompilerParams(dimension_semantics=("arbitrary",),
                                             vmem_limit_bytes=VMEM_LIMIT),
        name="inproj",
    )(x2, sh1, sc1, norm_g, w_all, w_gate, gate_bias, conv_w, conv_b, ln_g, ln_b)


def _cumsum_rows(x):
    n = x.shape[0]
    row = lax.broadcasted_iota(jnp.int32, x.shape, 0)
    k = 1
    while k < n:
        x = x + jnp.where(row >= k, pltpu.roll(x, k, axis=0), 0.0)
        k *= 2
    return x


def _rep2(a):
    return jnp.concatenate([a, a], axis=1)


def _mixer_kernel(x_ref, g1_ref, u_ref, v_ref, q_ref, kt_ref, mv_ref, o_ref, ga_ref, gb_ref,
                  gate_ref, ws_ref, bsb_ref, hng_ref, wout_ref,
                  h1_ref,
                  c_scr, n_scr, m_scr, merged_scr):
    L = L_MIX
    j = pl.program_id(1)

    @pl.when(j == 0)
    def _():
        c_scr[...] = jnp.zeros(c_scr.shape, F32)
        n_scr[...] = jnp.zeros(n_scr.shape, F32)
        m_scr[...] = jnp.zeros(m_scr.shape, F32)

    gates = gate_ref[...]
    bcum = _cumsum_rows(gates)
    gates_t = gates.T
    bcum_t = bcum.T
    row = lax.broadcasted_iota(jnp.int32, (L, L), 0)
    col = lax.broadcasted_iota(jnp.int32, (L, L), 1)
    causal = col <= row
    ones_rhs = jnp.ones((L, LANES), BF16)

    brow = lax.broadcasted_iota(jnp.int32, (GMLP_BLOCK, GMLP_BLOCK), 0) // 64
    bcol = lax.broadcasted_iota(jnp.int32, (GMLP_BLOCK, GMLP_BLOCK), 1) // 64
    wmask = bcol <= brow
    nblk = L // GMLP_BLOCK

    def rowsum(a):
        return jnp.dot(a.astype(BF16), ones_rhs, preferred_element_type=F32)

    for h in range(HEADS):
        hs = slice(h * HEAD_DIM, (h + 1) * HEAD_DIM)
        li_r = gates_t[h:h + 1, :]
        b_r = bcum_t[HEADS + h:HEADS + h + 1, :]
        b_c = jnp.broadcast_to(bcum[:, HEADS + h:HEADS + h + 1], (L, LANES))
        m_prev = m_scr[h, 0:1, 0:1]
        dmat = jnp.where(causal, (_rep2(b_c) - b_r) + li_r, -jnp.inf)
        m_inter = b_c + m_prev
        m_t = jnp.maximum(m_inter, jnp.max(dmat, axis=-1, keepdims=True))
        wts = jnp.exp(dmat - _rep2(m_t))
        qh = q_ref[h]
        kt = kt_ref[h]
        vh = mv_ref[h]
        s = jnp.dot(qh, kt, preferred_element_type=F32) * wts
        sb = s.astype(BF16)
        inter = jnp.exp(m_inter - m_t)
        c_old = c_scr[h]
        n_old = n_scr[h]
        n_hi = n_old.astype(BF16)
        n_lo = (n_old - n_hi.astype(F32)).astype(BF16)
        qn2 = jnp.dot(qh, jnp.concatenate([n_hi, n_lo], axis=1), preferred_element_type=F32)
        qn = qn2[:, :LANES] + qn2[:, LANES:]
        num = (jnp.dot(sb, vh, preferred_element_type=F32)
               + _rep2(inter) * jnp.dot(qh, c_old.astype(BF16), preferred_element_type=F32))
        den = jnp.dot(sb, ones_rhs, preferred_element_type=F32) + inter * qn
        inv = 1.0 / jnp.maximum(jnp.abs(den), jnp.exp(-m_t))
        hh = num * _rep2(inv)

        b_last = b_r[:, L - 1:L]
        g_r = (b_last - b_r) + li_r
        m_new = jnp.maximum(b_last + m_prev, jnp.max(g_r, axis=-1, keepdims=True))
        decay = jnp.exp((b_last + m_prev) - m_new)
        wkt = (jnp.exp(g_r - m_new) * kt.astype(F32)).astype(BF16)
        c_scr[h] = decay * c_old + jnp.dot(wkt, vh, preferred_element_type=F32)
        n_scr[h] = decay * n_old + jnp.dot(wkt, ones_rhs, preferred_element_type=F32)
        m_scr[h] = jnp.broadcast_to(m_new, (SUBLANES, LANES))

        mu = rowsum(hh) * (1.0 / HEAD_DIM)
        dh = hh - _rep2(mu)
        var = rowsum(dh * dh) * (1.0 / HEAD_DIM)
        y_b = (dh * _rep2(lax.rsqrt(var + EPS))) * hng_ref[:, hs] * o_ref[h].astype(F32)

        for half in range(CW // GROUP_DIM):
            g = h * (CW // GROUP_DIM) + half
            ls = slice(half * GROUP_DIM, (half + 1) * GROUP_DIM)
            wsg = jnp.where(wmask, ws_ref[g], 0.0).astype(BF16)
            vcat = jnp.concatenate(
                [v_ref[h, n * GMLP_BLOCK:(n + 1) * GMLP_BLOCK, ls] for n in range(nblk)], axis=1)
            mixed = jnp.dot(wsg, vcat, preferred_element_type=F32) + _rep2(bsb_ref[g])
            for n in range(nblk):
                rs = slice(n * GMLP_BLOCK, (n + 1) * GMLP_BLOCK)
                y_a = u_ref[h, rs, ls].astype(F32) * mixed[:, n * GMLP_BLOCK:(n + 1) * GMLP_BLOCK]
                merged = (ga_ref[h, rs, ls].astype(F32) * y_a
                          + gb_ref[h, rs, ls].astype(F32) * y_b[rs, ls])
                merged_scr[rs, g * GROUP_DIM:(g + 1) * GROUP_DIM] = merged.astype(BF16)

    proj = jnp.dot(merged_scr[...], wout_ref[...], preferred_element_type=F32)
    h1_ref[...] = x_ref[...] + g1_ref[...] * proj


def _mixer(x2, g1, u, v, q, kt, mv, o, ga, gb, gates, ws, bs_rep, hn_g, w_out):
    nsb = SEQ // L_MIX
    row = lambda b, j: (b * nsb + j, 0)
    seg_in = pl.BlockSpec((NCH, L_MIX, CW), lambda b, j: (0, b * nsb + j, 0))
    kt_in = pl.BlockSpec((NCH, CW, L_MIX), lambda b, j: (0, 0, b * nsb + j))
    return pl.pallas_call(
        _mixer_kernel,
        grid=(BATCH, nsb),
        in_specs=[pl.BlockSpec((L_MIX, D), row),
                  pl.BlockSpec((None, 1, D), lambda b, j: (b, 0, 0)),
                  seg_in, seg_in, seg_in, kt_in, seg_in, seg_in, seg_in, seg_in,
                  pl.BlockSpec((L_MIX, LANES), row),
                  _resident((GMLP_GROUPS, GMLP_BLOCK, GMLP_BLOCK)),
                  _resident((GMLP_GROUPS, GMLP_BLOCK, LANES)),
                  _resident((1, D)),
                  _resident((D, D))],
        out_specs=pl.BlockSpec((L_MIX, D), row),
        out_shape=jax.ShapeDtypeStruct((TOKENS, D), F32),
        scratch_shapes=[pltpu.VMEM((HEADS, HEAD_DIM, HEAD_DIM), F32),
                        pltpu.VMEM((HEADS, HEAD_DIM, LANES), F32),
                        pltpu.VMEM((HEADS, SUBLANES, LANES), F32),
                        pltpu.VMEM((L_MIX, D), BF16)],
        compiler_params=pltpu.CompilerParams(dimension_semantics=("arbitrary", "arbitrary"),
                                             vmem_limit_bytes=VMEM_LIMIT),
        name="mixer",
    )(x2, g1, u, v, q, kt, mv, o, ga, gb, gates, ws, bs_rep, hn_g, w_out)


def _ffn_kernel(h_ref, sh_ref, sc_ref, g2_ref, ng_ref, w1_ref, w2_ref, fg_ref, o_ref):
    h = h_ref[...]
    xn = _rms_mod(h, ng_ref[...], sh_ref[...], sc_ref[...]).astype(BF16)
    acc = jnp.zeros((TM_FFN, D), F32)
    for c in range(FFN // FFN_CHUNK):
        cs = slice(c * FFN_CHUNK, (c + 1) * FFN_CHUNK)
        a = jnp.maximum(jnp.dot(xn, w1_ref[:, cs], preferred_element_type=F32), 0.0)
        acc = acc + jnp.dot((a * a).astype(BF16), w2_ref[cs, :], preferred_element_type=F32)
    h2 = h + g2_ref[...] * acc
    y = h2 * lax.rsqrt(jnp.mean(h2 * h2, axis=-1, keepdims=True) + EPS)
    o_ref[...] = y * fg_ref[...]


def _ffn(h1, sh2, sc2, g2, norm_g, w1, w2, final_g):
    nsb = SEQ // TM_FFN
    row = lambda i: (i, 0)
    per_batch = lambda i: (i // nsb, 0, 0)
    return pl.pallas_call(
        _ffn_kernel,
        grid=(TOKENS // TM_FFN,),
        in_specs=[pl.BlockSpec((TM_FFN, D), row),
                  pl.BlockSpec((None, 1, D), per_batch),
                  pl.BlockSpec((None, 1, D), per_batch),
                  pl.BlockSpec((None, 1, D), per_batch),
                  _resident((1, D)),
                  _resident((D, FFN)),
                  _resident((FFN, D)),
                  _resident((1, D))],
        out_specs=pl.BlockSpec((TM_FFN, D), row),
        out_shape=jax.ShapeDtypeStruct((TOKENS, D), F32),
        compiler_params=pltpu.CompilerParams(dimension_semantics=("arbitrary",),
                                             vmem_limit_bytes=VMEM_LIMIT),
        name="ffn",
    )(h1, sh2, sc2, g2, norm_g, w1, w2, final_g)


def kernel(x, c, w_ada, b_ada, norm1_g, w_in, conv_w, conv_b, mlstm_gate_b, gmlp_ln_g, gmlp_ln_b,
           gmlp_ws, gmlp_bs, mlstm_hn_g, w_out, norm2_g, w_ff1, w_ff2, final_g):
    l = 0
    mod = _ada(c, w_ada[l], b_ada[l]).reshape(BATCH, 6, 1, D)
    sh1, sc1, g1, sh2, sc2, g2 = (mod[:, t] for t in range(6))

    w = w_in[l]
    w_segs = jnp.concatenate([w[:, :OFF_I], w[:, OFF_GA:]], axis=1).astype(BF16)
    w_all = w_segs.reshape(D, N_SEG, NCH, CW).transpose(1, 2, 0, 3)
    w_gate = jnp.pad(w[:, OFF_I:OFF_GA], ((0, 0), (0, LANES - 2 * HEADS))).astype(BF16)
    gate_bias = jnp.pad(mlstm_gate_b[l].reshape(1, 2 * HEADS), ((0, 0), (0, LANES - 2 * HEADS)))
    cw = conv_w[l].reshape(CONV_K, 2, NCH, CW).transpose(1, 2, 0, 3)
    cb = conv_b[l].reshape(2, NCH, 1, CW)

    x2 = x.reshape(TOKENS, D)
    u, v, q, kt, mv, o, ga, gb, gates = _inproj(
        x2, sh1, sc1, norm1_g[l].reshape(1, D), w_all, w_gate, gate_bias, cw, cb,
        gmlp_ln_g[l].reshape(NCH, 1, CW), gmlp_ln_b[l].reshape(NCH, 1, CW))

    bs_rep = jnp.broadcast_to(gmlp_bs[l][:, :, None], (GMLP_GROUPS, GMLP_BLOCK, LANES))
    h1 = _mixer(x2, g1, u, v, q, kt, mv, o, ga, gb, gates,
                gmlp_ws[l], bs_rep, mlstm_hn_g[l].reshape(1, D), w_out[l].astype(BF16))

    out = _ffn(h1, sh2, sc2, g2, norm2_g[l].reshape(1, D),
               w_ff1[l].astype(BF16), w_ff2[l].astype(BF16), final_g.reshape(1, D))
    return out.reshape(BATCH, SEQ, D)
```

```python
import jax
import jax.numpy as jnp
from jax import lax
from jax.experimental import pallas as pl
from jax.experimental.pallas import tpu as pltpu

D = 1024
BATCH = 16
SEQ = 2048
TOKENS = BATCH * SEQ
GMLP_BLOCK = 128
GMLP_GROUPS = 8
GROUP_DIM = D // GMLP_GROUPS
HEADS = 4
HEAD_DIM = 256
CONV_K = 4
FFN = 4 * D
EPS = 1e-6
LANES = 128
SUBLANES = 8

OFF_I = 6 * D
OFF_GA = OFF_I + 2 * HEADS
N_SEG = 8
SEG_U, SEG_V, SEG_Q, SEG_K, SEG_MV, SEG_O, SEG_GA, SEG_GB = range(N_SEG)
CW = 256
NCH = D // CW

TM_IN = 512
ROWS_PIECE = 256
L_MIX = 256
TM_FFN = 512
FFN_CHUNK = 1024
ADA_COLS = 1536

F32 = jnp.float32
BF16 = jnp.bfloat16
VMEM_LIMIT = 56 * 1024 * 1024


def _resident(shape):
    nd = len(shape)
    return pl.BlockSpec(shape, lambda *_: (0,) * nd, pipeline_mode=pl.Buffered(1))


def _ada_kernel(c_ref, w_ref, b_ref, o_ref):
    c = c_ref[...]
    ca = c * jax.nn.sigmoid(c)
    o_ref[...] = jnp.dot(ca, w_ref[...], preferred_element_type=F32,
                         precision=lax.Precision.HIGHEST) + b_ref[...]


def _ada(c, w, b):
    n = w.shape[1]
    return pl.pallas_call(
        _ada_kernel,
        grid=(n // ADA_COLS,),
        in_specs=[pl.BlockSpec((BATCH, D), lambda j: (0, 0)),
                  pl.BlockSpec((D, ADA_COLS), lambda j: (0, j)),
                  pl.BlockSpec((1, ADA_COLS), lambda j: (0, j))],
        out_specs=pl.BlockSpec((BATCH, ADA_COLS), lambda j: (0, j)),
        out_shape=jax.ShapeDtypeStruct((BATCH, n), F32),
        compiler_params=pltpu.CompilerParams(vmem_limit_bytes=VMEM_LIMIT),
        name="ada",
    )(c, w, b.reshape(1, n))


def _rms_mod(x, g, shift, scale):
    y = x * lax.rsqrt(jnp.mean(x * x, axis=-1, keepdims=True) + EPS)
    return (y * g) * (1.0 + scale) + shift


def _conv_silu(z, halo_ref, c, w, b):
    tm = z.shape[0]
    zz = jnp.concatenate([halo_ref[c], z], axis=0)
    acc = b + w[CONV_K - 1:CONV_K, :] * z
    for j in range(CONV_K - 1):
        shifted = pltpu.roll(zz, CONV_K - 1 - j, axis=0)[SUBLANES:, :]
        acc = acc + w[j:j + 1, :] * shifted
    halo_ref[c] = z[tm - SUBLANES:, :]
    return acc * jax.nn.sigmoid(acc)


def _inproj_kernel(x_ref, sh_ref, sc_ref, g_ref, w_ref, wg_ref, gb_ref, cw_ref, cb_ref,
                   lng_ref, lnb_ref,
                   u_ref, v_ref, q_ref, kt_ref, mv_ref, o_ref, ga_ref, gbo_ref, gate_ref,
                   xn_scr, gv_scr, s1_scr, halo_scr):
    i = pl.program_id(0)

    @pl.when((i % (SEQ // TM_IN)) == 0)
    def _():
        halo_scr[...] = jnp.zeros(halo_scr.shape, F32)

    xn_scr[...] = _rms_mod(x_ref[...], g_ref[...], sh_ref[...], sc_ref[...]).astype(BF16)
    s1_scr[...] = jnp.zeros(s1_scr.shape, F32)

    def seg(k, c, rs):
        return jnp.dot(xn_scr[rs, :], w_ref[k, c], preferred_element_type=F32)

    pieces = [slice(p * ROWS_PIECE, (p + 1) * ROWS_PIECE) for p in range(TM_IN // ROWS_PIECE)]

    def loop_main(c, carry):
        for rs in pieces:
            k = _conv_silu(seg(SEG_K, c, rs), halo_scr.at[1], c, cw_ref[1, c], cb_ref[1, c])
            kt_ref[c, :, rs] = k.T.astype(BF16)
            gv = jax.nn.gelu(seg(SEG_V, c, rs))
            gv_scr[c, rs, :] = gv
            s1_scr[rs, :] += gv[:, :LANES] + gv[:, LANES:]
            q = _conv_silu(seg(SEG_Q, c, rs), halo_scr.at[0], c, cw_ref[0, c], cb_ref[0, c])
            q_ref[c, rs, :] = (q * (HEAD_DIM ** -0.5)).astype(BF16)
            u_ref[c, rs, :] = jax.nn.gelu(seg(SEG_U, c, rs)).astype(BF16)
            o_ref[c, rs, :] = jax.nn.sigmoid(seg(SEG_O, c, rs)).astype(BF16)
            ga_ref[c, rs, :] = jax.nn.sigmoid(seg(SEG_GA, c, rs)).astype(BF16)
            gbo_ref[c, rs, :] = jax.nn.sigmoid(seg(SEG_GB, c, rs)).astype(BF16)
            mv_ref[c, rs, :] = seg(SEG_MV, c, rs).astype(BF16)
        return carry

    lax.fori_loop(0, NCH, loop_main, 0)

    gp = jnp.dot(xn_scr[...], wg_ref[...], preferred_element_type=F32) + gb_ref[...]
    lane = lax.broadcasted_iota(jnp.int32, gp.shape, 1)
    gate_ref[...] = jnp.where(lane < HEADS, gp, jax.nn.log_sigmoid(gp))

    mu = jnp.sum(s1_scr[...], axis=-1, keepdims=True) * (1.0 / D)
    ssq = jnp.zeros((TM_IN, LANES), F32)
    for c in range(NCH):
        dv = gv_scr[c] - mu
        dv = dv * dv
        ssq = ssq + (dv[:, :LANES] + dv[:, LANES:])
    rstd = lax.rsqrt(jnp.sum(ssq, axis=-1, keepdims=True) * (1.0 / D) + EPS)
    for c in range(NCH):
        v_ref[c] = (((gv_scr[c] - mu) * rstd) * lng_ref[c] + lnb_ref[c]).astype(BF16)


def _inproj(x2, sh1, sc1, norm_g, w_all, w_gate, gate_bias, conv_w, conv_b, ln_g, ln_b):
    nsb = SEQ // TM_IN
    row = lambda i: (i, 0)
    per_batch = lambda i: (i // nsb, 0, 0)
    seg_out = pl.BlockSpec((NCH, TM_IN, CW), lambda i: (0, i, 0))
    seg_shape = jax.ShapeDtypeStruct((NCH, TOKENS, CW), BF16)
    seg_outs = [seg_out] * N_SEG
    seg_shapes = [seg_shape] * N_SEG
    seg_outs[SEG_K] = pl.BlockSpec((NCH, CW, TM_IN), lambda i: (0, 0, i))
    seg_shapes[SEG_K] = jax.ShapeDtypeStruct((NCH, CW, TOKENS), BF16)
    return pl.pallas_call(
        _inproj_kernel,
        grid=(TOKENS // TM_IN,),
        in_specs=[pl.BlockSpec((TM_IN, D), row),
                  pl.BlockSpec((None, 1, D), per_batch),
                  pl.BlockSpec((None, 1, D), per_batch),
                  _resident((1, D)),
                  _resident((N_SEG, NCH, D, CW)),
                  _resident((D, LANES)),
                  _resident((1, LANES)),
                  _resident((2, NCH, CONV_K, CW)),
                  _resident((2, NCH, 1, CW)),
                  _resident((NCH, 1, CW)),
                  _resident((NCH, 1, CW))],
        out_specs=seg_outs + [pl.BlockSpec((TM_IN, LANES), row)],
        out_shape=seg_shapes + [jax.ShapeDtypeStruct((TOKENS, LANES), F32)],
        scratch_shapes=[pltpu.VMEM((TM_IN, D), BF16),
                        pltpu.VMEM((NCH, TM_IN, CW), F32),
                        pltpu.VMEM((TM_IN, LANES), F32),
                        pltpu.VMEM((2, NCH, SUBLANES, CW), F32)],
        compiler_params=pltpu.CompilerParams(dimension_semantics=("arbitrary",),
                                             vmem_limit_bytes=VMEM_LIMIT),
        name="inproj",
    )(x2, sh1, sc1, norm_g, w_all, w_gate, gate_bias, conv_w, conv_b, ln_g, ln_b)


def _cumsum_rows(x):
    n = x.shape[0]
    row = lax.broadcasted_iota(jnp.int32, x.shape, 0)
    k = 1
    while k < n:
        x = x + jnp.where(row >= k, pltpu.roll(x, k, axis=0), 0.0)
        k *= 2
    return x


def _rep2(a):
    return jnp.concatenate([a, a], axis=1)


def _mixer_kernel(x_ref, g1_ref, u_ref, v_ref, q_ref, kt_ref, mv_ref, o_ref, ga_ref, gb_ref,
                  gate_ref, ws_ref, bsb_ref, hng_ref, wout_ref,
                  h1_ref,
                  c_scr, n_scr, m_scr, merged_scr):
    L = L_MIX
    j = pl.program_id(1)

    @pl.when(j == 0)
    def _():
        c_scr[...] = jnp.zeros(c_scr.shape, F32)
        n_scr[...] = jnp.zeros(n_scr.shape, F32)
        m_scr[...] = jnp.zeros(m_scr.shape, F32)

    gates = gate_ref[...]
    bcum = _cumsum_rows(gates)
    gates_t = gates.T
    bcum_t = bcum.T
    row = lax.broadcasted_iota(jnp.int32, (L, L), 0)
    col = lax.broadcasted_iota(jnp.int32, (L, L), 1)
    causal = col <= row
    ones_rhs = jnp.ones((L, LANES), BF16)

    brow = lax.broadcasted_iota(jnp.int32, (GMLP_BLOCK, GMLP_BLOCK), 0) // 64
    bcol = lax.broadcasted_iota(jnp.int32, (GMLP_BLOCK, GMLP_BLOCK), 1) // 64
    wmask = bcol <= brow
    nblk = L // GMLP_BLOCK

    def rowsum(a):
        return jnp.dot(a.astype(BF16), ones_rhs, preferred_element_type=F32)

    for h in range(HEADS):
        hs = slice(h * HEAD_DIM, (h + 1) * HEAD_DIM)
        li_r = gates_t[h:h + 1, :]
        b_r = bcum_t[HEADS + h:HEADS + h + 1, :]
        b_c = jnp.broadcast_to(bcum[:, HEADS + h:HEADS + h + 1], (L, LANES))
        m_prev = m_scr[h, 0:1, 0:1]
        dmat = jnp.where(causal, (_rep2(b_c) - b_r) + li_r, -jnp.inf)
        m_inter = b_c + m_prev
        m_t = jnp.maximum(m_inter, jnp.max(dmat, axis=-1, keepdims=True))
        wts = jnp.exp(dmat - _rep2(m_t))
        qh = q_ref[h]
        kt = kt_ref[h]
        vh = mv_ref[h]
        s = jnp.dot(qh, kt, preferred_element_type=F32) * wts
        sb = s.astype(BF16)
        inter = jnp.exp(m_inter - m_t)
        c_old = c_scr[h]
        n_old = n_scr[h]
        n_hi = n_old.astype(BF16)
        n_lo = (n_old - n_hi.astype(F32)).astype(BF16)
        qn2 = jnp.dot(qh, jnp.concatenate([n_hi, n_lo], axis=1), preferred_element_type=F32)
        qn = qn2[:, :LANES] + qn2[:, LANES:]
        num = (jnp.dot(sb, vh, preferred_element_type=F32)
               + _rep2(inter) * jnp.dot(qh, c_old.astype(BF16), preferred_element_type=F32))
        den = jnp.dot(sb, ones_rhs, preferred_element_type=F32) + inter * qn
        inv = 1.0 / jnp.maximum(jnp.abs(den), jnp.exp(-m_t))
        hh = num * _rep2(inv)

        b_last = b_r[:, L - 1:L]
        g_r = (b_last - b_r) + li_r
        m_new = jnp.maximum(b_last + m_prev, jnp.max(g_r, axis=-1, keepdims=True))
        decay = jnp.exp((b_last + m_prev) - m_new)
        wkt = (jnp.exp(g_r - m_new) * kt.astype(F32)).astype(BF16)
        c_scr[h] = decay * c_old + jnp.dot(wkt, vh, preferred_element_type=F32)
        n_scr[h] = decay * n_old + jnp.dot(wkt, ones_rhs, preferred_element_type=F32)
        m_scr[h] = jnp.broadcast_to(m_new, (SUBLANES, LANES))

        mu = rowsum(hh) * (1.0 / HEAD_DIM)
        dh = hh - _rep2(mu)
        var = rowsum(dh * dh) * (1.0 / HEAD_DIM)
        y_b = (dh * _rep2(lax.rsqrt(var + EPS))) * hng_ref[:, hs] * o_ref[h].astype(F32)

        for half in range(CW // GROUP_DIM):
            g = h * (CW // GROUP_DIM) + half
            ls = slice(half * GROUP_DIM, (half + 1) * GROUP_DIM)
            wsg = jnp.where(wmask, ws_ref[g], 0.0).astype(BF16)
            vcat = jnp.concatenate(
                [v_ref[h, n * GMLP_BLOCK:(n + 1) * GMLP_BLOCK, ls] for n in range(nblk)], axis=1)
            mixed = jnp.dot(wsg, vcat, preferred_element_type=F32) + _rep2(bsb_ref[g])
            for n in range(nblk):
                rs = slice(n * GMLP_BLOCK, (n + 1) * GMLP_BLOCK)
                y_a = u_ref[h, rs, ls].astype(F32) * mixed[:, n * GMLP_BLOCK:(n + 1) * GMLP_BLOCK]
                merged = (ga_ref[h, rs, ls].astype(F32) * y_a
                          + gb_ref[h, rs, ls].astype(F32) * y_b[rs, ls])
                merged_scr[rs, g * GROUP_DIM:(g + 1) * GROUP_DIM] = merged.astype(BF16)

    proj = jnp.dot(merged_scr[...], wout_ref[...], preferred_element_type=F32)
    h1_ref[...] = x_ref[...] + g1_ref[...] * proj


def _mixer(x2, g1, u, v, q, kt, mv, o, ga, gb, gates, ws, bs_rep, hn_g, w_out):
    nsb = SEQ // L_MIX
    row = lambda b, j: (b * nsb + j, 0)
    seg_in = pl.BlockSpec((NCH, L_MIX, CW), lambda b, j: (0, b * nsb + j, 0))
    kt_in = pl.BlockSpec((NCH, CW, L_MIX), lambda b, j: (0, 0, b * nsb + j))
    return pl.pallas_call(
        _mixer_kernel,
        grid=(BATCH, nsb),
        in_specs=[pl.BlockSpec((L_MIX, D), row),
                  pl.BlockSpec((None, 1, D), lambda b, j: (b, 0, 0)),
                  seg_in, seg_in, seg_in, kt_in, seg_in, seg_in, seg_in, seg_in,
                  pl.BlockSpec((L_MIX, LANES), row),
                  _resident((GMLP_GROUPS, GMLP_BLOCK, GMLP_BLOCK)),
                  _resident((GMLP_GROUPS, GMLP_BLOCK, LANES)),
                  _resident((1, D)),
                  _resident((D, D))],
        out_specs=pl.BlockSpec((L_MIX, D), row),
        out_shape=jax.ShapeDtypeStruct((TOKENS, D), F32),
        scratch_shapes=[pltpu.VMEM((HEADS, HEAD_DIM, HEAD_DIM), F32),
                        pltpu.VMEM((HEADS, HEAD_DIM, LANES), F32),
                        pltpu.VMEM((HEADS, SUBLANES, LANES), F32),
                        pltpu.VMEM((L_MIX, D), BF16)],
        compiler_params=pltpu.CompilerParams(dimension_semantics=("arbitrary", "arbitrary"),
                                             vmem_limit_bytes=VMEM_LIMIT),
        name="mixer",
    )(x2, g1, u, v, q, kt, mv, o, ga, gb, gates, ws, bs_rep, hn_g, w_out)


def _ffn_kernel(h_ref, sh_ref, sc_ref, g2_ref, ng_ref, w1_ref, w2_ref, fg_ref, o_ref):
    h = h_ref[...]
    xn = _rms_mod(h, ng_ref[...], sh_ref[...], sc_ref[...]).astype(BF16)
    acc = jnp.zeros((TM_FFN, D), F32)
    for c in range(FFN // FFN_CHUNK):
        cs = slice(c * FFN_CHUNK, (c + 1) * FFN_CHUNK)
        a = jnp.maximum(jnp.dot(xn, w1_ref[:, cs], preferred_element_type=F32), 0.0)
        acc = acc + jnp.dot((a * a).astype(BF16), w2_ref[cs, :], preferred_element_type=F32)
    h2 = h + g2_ref[...] * acc
    y = h2 * lax.rsqrt(jnp.mean(h2 * h2, axis=-1, keepdims=True) + EPS)
    o_ref[...] = y * fg_ref[...]


def _ffn(h1, sh2, sc2, g2, norm_g, w1, w2, final_g):
    nsb = SEQ // TM_FFN
    row = lambda i: (i, 0)
    per_batch = lambda i: (i // nsb, 0, 0)
    return pl.pallas_call(
        _ffn_kernel,
        grid=(TOKENS // TM_FFN,),
        in_specs=[pl.BlockSpec((TM_FFN, D), row),
                  pl.BlockSpec((None, 1, D), per_batch),
                  pl.BlockSpec((None, 1, D), per_batch),
                  pl.BlockSpec((None, 1, D), per_batch),
                  _resident((1, D)),
                  _resident((D, FFN)),
                  _resident((FFN, D)),
                  _resident((1, D))],
        out_specs=pl.BlockSpec((TM_FFN, D), row),
        out_shape=jax.ShapeDtypeStruct((TOKENS, D), F32),
        compiler_params=pltpu.CompilerParams(dimension_semantics=("arbitrary",),
                                             vmem_limit_bytes=VMEM_LIMIT),
        name="ffn",
    )(h1, sh2, sc2, g2, norm_g, w1, w2, final_g)


def kernel(x, c, w_ada, b_ada, norm1_g, w_in, conv_w, conv_b, mlstm_gate_b, gmlp_ln_g, gmlp_ln_b,
           gmlp_ws, gmlp_bs, mlstm_hn_g, w_out, norm2_g, w_ff1, w_ff2, final_g):
    l = 0
    mod = _ada(c, w_ada[l], b_ada[l]).reshape(BATCH, 6, 1, D)
    sh1, sc1, g1, sh2, sc2, g2 = (mod[:, t] for t in range(6))

    w = w_in[l]
    w_segs = jnp.concatenate([w[:, :OFF_I], w[:, OFF_GA:]], axis=1).astype(BF16)
    w_all = w_segs.reshape(D, N_SEG, NCH, CW).transpose(1, 2, 0, 3)
    w_gate = jnp.pad(w[:, OFF_I:OFF_GA], ((0, 0), (0, LANES - 2 * HEADS))).astype(BF16)
    gate_bias = jnp.pad(mlstm_gate_b[l].reshape(1, 2 * HEADS), ((0, 0), (0, LANES - 2 * HEADS)))
    cw = conv_w[l].reshape(CONV_K, 2, NCH, CW).transpose(1, 2, 0, 3)
    cb = conv_b[l].reshape(2, NCH, 1, CW)

    x2 = x.reshape(TOKENS, D)
    u, v, q, kt, mv, o, ga, gb, gates = _inproj(
        x2, sh1, sc1, norm1_g[l].reshape(1, D), w_all, w_gate, gate_bias, cw, cb,
        gmlp_ln_g[l].reshape(NCH, 1, CW), gmlp_ln_b[l].reshape(NCH, 1, CW))

    bs_rep = jnp.broadcast_to(gmlp_bs[l][:, :, None], (GMLP_GROUPS, GMLP_BLOCK, LANES))
    h1 = _mixer(x2, g1, u, v, q, kt, mv, o, ga, gb, gates,
                gmlp_ws[l], bs_rep, mlstm_hn_g[l].reshape(1, D), w_out[l].astype(BF16))

    out = _ffn(h1, sh2, sc2, g2, norm2_g[l].reshape(1, D),
               w_ff1[l].astype(BF16), w_ff2[l].astype(BF16), final_g.reshape(1, D))
    return out.reshape(BATCH, SEQ, D)
```

```python
import jax
import jax.numpy as jnp
from jax import lax
from jax.experimental import pallas as pl
from jax.experimental.pallas import tpu as pltpu

D = 1024
BATCH = 16
SEQ = 2048
TOKENS = BATCH * SEQ
GMLP_BLOCK = 128
GMLP_GROUPS = 8
GROUP_DIM = D // GMLP_GROUPS
HEADS = 4
HEAD_DIM = 256
CONV_K = 4
FFN = 4 * D
EPS = 1e-6
LANES = 128
SUBLANES = 8

OFF_I = 6 * D
OFF_GA = OFF_I + 2 * HEADS
N_SEG = 8
SEG_U, SEG_V, SEG_Q, SEG_K, SEG_MV, SEG_O, SEG_GA, SEG_GB = range(N_SEG)
CW = 256
NCH = D // CW

TM_IN = 512
ROWS_PIECE = 512
L_MIX = 256
TM_FFN = 512
FFN_CHUNK = 1024
ADA_COLS = 1536

F32 = jnp.float32
BF16 = jnp.bfloat16
VMEM_LIMIT = 56 * 1024 * 1024


def _resident(shape):
    nd = len(shape)
    return pl.BlockSpec(shape, lambda *_: (0,) * nd, pipeline_mode=pl.Buffered(1))


def _ada_kernel(c_ref, w_ref, b_ref, o_ref):
    c = c_ref[...]
    ca = c * jax.nn.sigmoid(c)
    o_ref[...] = jnp.dot(ca, w_ref[...], preferred_element_type=F32,
                         precision=lax.Precision.HIGHEST) + b_ref[...]


def _ada(c, w, b):
    n = w.shape[1]
    return pl.pallas_call(
        _ada_kernel,
        grid=(n // ADA_COLS,),
        in_specs=[pl.BlockSpec((BATCH, D), lambda j: (0, 0)),
                  pl.BlockSpec((D, ADA_COLS), lambda j: (0, j)),
                  pl.BlockSpec((1, ADA_COLS), lambda j: (0, j))],
        out_specs=pl.BlockSpec((BATCH, ADA_COLS), lambda j: (0, j)),
        out_shape=jax.ShapeDtypeStruct((BATCH, n), F32),
        compiler_params=pltpu.CompilerParams(vmem_limit_bytes=VMEM_LIMIT),
        name="ada",
    )(c, w, b.reshape(1, n))


def _rms_mod(x, g, shift, scale):
    y = x * lax.rsqrt(jnp.mean(x * x, axis=-1, keepdims=True) + EPS)
    return (y * g) * (1.0 + scale) + shift


def _conv_silu(z, halo_ref, c, w, b):
    tm = z.shape[0]
    zz = jnp.concatenate([halo_ref[c], z], axis=0)
    acc = b + w[CONV_K - 1:CONV_K, :] * z
    for j in range(CONV_K - 1):
        shifted = pltpu.roll(zz, CONV_K - 1 - j, axis=0)[SUBLANES:, :]
        acc = acc + w[j:j + 1, :] * shifted
    halo_ref[c] = z[tm - SUBLANES:, :]
    return acc * jax.nn.sigmoid(acc)


def _inproj_kernel(x_ref, sh_ref, sc_ref, g_ref, w_ref, wg_ref, gb_ref, cw_ref, cb_ref,
                   lng_ref, lnb_ref,
                   u_ref, v_ref, q_ref, kt_ref, mv_ref, o_ref, ga_ref, gbo_ref, gate_ref,
                   xn_scr, gv_scr, s1_scr, halo_scr):
    i = pl.program_id(0)

    @pl.when((i % (SEQ // TM_IN)) == 0)
    def _():
        halo_scr[...] = jnp.zeros(halo_scr.shape, F32)

    xn_scr[...] = _rms_mod(x_ref[...], g_ref[...], sh_ref[...], sc_ref[...]).astype(BF16)
    s1_scr[...] = jnp.zeros(s1_scr.shape, F32)

    def seg(k, c, rs):
        return jnp.dot(xn_scr[rs, :], w_ref[k, c], preferred_element_type=F32)

    pieces = [slice(p * ROWS_PIECE, (p + 1) * ROWS_PIECE) for p in range(TM_IN // ROWS_PIECE)]

    def loop_main(c, carry):
        for rs in pieces:
            k = _conv_silu(seg(SEG_K, c, rs), halo_scr.at[1], c, cw_ref[1, c], cb_ref[1, c])
            kt_ref[c, :, rs] = k.T.astype(BF16)
            gv = jax.nn.gelu(seg(SEG_V, c, rs))
            gv_scr[c, rs, :] = gv
            s1_scr[rs, :] += gv[:, :LANES] + gv[:, LANES:]
            q = _conv_silu(seg(SEG_Q, c, rs), halo_scr.at[0], c, cw_ref[0, c], cb_ref[0, c])
            q_ref[c, rs, :] = (q * (HEAD_DIM ** -0.5)).astype(BF16)
            u_ref[c, rs, :] = jax.nn.gelu(seg(SEG_U, c, rs)).astype(BF16)
            o_ref[c, rs, :] = jax.nn.sigmoid(seg(SEG_O, c, rs)).astype(BF16)
            ga_ref[c, rs, :] = jax.nn.sigmoid(seg(SEG_GA, c, rs)).astype(BF16)
            gbo_ref[c, rs, :] = jax.nn.sigmoid(seg(SEG_GB, c, rs)).astype(BF16)
            mv_ref[c, rs, :] = seg(SEG_MV, c, rs).astype(BF16)
        return carry

    lax.fori_loop(0, NCH, loop_main, 0)

    gp = jnp.dot(xn_scr[...], wg_ref[...], preferred_element_type=F32) + gb_ref[...]
    lane = lax.broadcasted_iota(jnp.int32, gp.shape, 1)
    gate_ref[...] = jnp.where(lane < HEADS, gp, jax.nn.log_sigmoid(gp))

    mu = jnp.sum(s1_scr[...], axis=-1, keepdims=True) * (1.0 / D)
    ssq = jnp.zeros((TM_IN, LANES), F32)
    for c in range(NCH):
        dv = gv_scr[c] - mu
        dv = dv * dv
        ssq = ssq + (dv[:, :LANES] + dv[:, LANES:])
    rstd = lax.rsqrt(jnp.sum(ssq, axis=-1, keepdims=True) * (1.0 / D) + EPS)
    for c in range(NCH):
        v_ref[c] = (((gv_scr[c] - mu) * rstd) * lng_ref[c] + lnb_ref[c]).astype(BF16)


def _inproj(x2, sh1, sc1, norm_g, w_all, w_gate, gate_bias, conv_w, conv_b, ln_g, ln_b):
    nsb = SEQ // TM_IN
    row = lambda i: (i, 0)
    per_batch = lambda i: (i // nsb, 0, 0)
    seg_out = pl.BlockSpec((NCH, TM_IN, CW), lambda i: (0, i, 0))
    seg_shape = jax.ShapeDtypeStruct((NCH, TOKENS, CW), BF16)
    seg_outs = [seg_out] * N_SEG
    seg_shapes = [seg_shape] * N_SEG
    seg_outs[SEG_K] = pl.BlockSpec((NCH, CW, TM_IN), lambda i: (0, 0, i))
    seg_shapes[SEG_K] = jax.ShapeDtypeStruct((NCH, CW, TOKENS), BF16)
    return pl.pallas_call(
        _inproj_kernel,
        grid=(TOKENS // TM_IN,),
        in_specs=[pl.BlockSpec((TM_IN, D), row),
                  pl.BlockSpec((None, 1, D), per_batch),
                  pl.BlockSpec((None, 1, D), per_batch),
                  _resident((1, D)),
                  _resident((N_SEG, NCH, D, CW)),
                  _resident((D, LANES)),
                  _resident((1, LANES)),
                  _resident((2, NCH, CONV_K, CW)),
                  _resident((2, NCH, 1, CW)),
                  _resident((NCH, 1, CW)),
                  _resident((NCH, 1, CW))],
        out_specs=seg_outs + [pl.BlockSpec((TM_IN, LANES), row)],
        out_shape=seg_shapes + [jax.ShapeDtypeStruct((TOKENS, LANES), F32)],
        scratch_shapes=[pltpu.VMEM((TM_IN, D), BF16),
                        pltpu.VMEM((NCH, TM_IN, CW), F32),
                        pltpu.VMEM((TM_IN, LANES), F32),
                        pltpu.VMEM((2, NCH, SUBLANES, CW), F32)],
        compiler_params=pltpu.CompilerParams(dimension_semantics=("arbitrary",),
                                             vmem_limit_bytes=VMEM_LIMIT),
        name="inproj",
    )(x2, sh1, sc1, norm_g, w_all, w_gate, gate_bias, conv_w, conv_b, ln_g, ln_b)


def _cumsum_rows(x):
    n = x.shape[0]
    row = lax.broadcasted_iota(jnp.int32, x.shape, 0)
    k = 1
    while k < n:
        x = x + jnp.where(row >= k, pltpu.roll(x, k, axis=0), 0.0)
        k *= 2
    return x


def _rep2(a):
    return jnp.concatenate([a, a], axis=1)


def _mixer_kernel(x_ref, g1_ref, u_ref, v_ref, q_ref, kt_ref, mv_ref, o_ref, ga_ref, gb_ref,
                  gate_ref, ws_ref, bsb_ref, hng_ref, wout_ref,
                  h1_ref,
                  c_scr, n_scr, m_scr, merged_scr):
    L = L_MIX
    j = pl.program_id(1)

    @pl.when(j == 0)
    def _():
        c_scr[...] = jnp.zeros(c_scr.shape, F32)
        n_scr[...] = jnp.zeros(n_scr.shape, F32)
        m_scr[...] = jnp.zeros(m_scr.shape, F32)

    gates = gate_ref[...]
    bcum = _cumsum_rows(gates)
    gates_t = gates.T
    bcum_t = bcum.T
    row = lax.broadcasted_iota(jnp.int32, (L, L), 0)
    col = lax.broadcasted_iota(jnp.int32, (L, L), 1)
    causal = col <= row
    ones_rhs = jnp.ones((L, LANES), BF16)

    brow = lax.broadcasted_iota(jnp.int32, (GMLP_BLOCK, GMLP_BLOCK), 0) // 64
    bcol = lax.broadcasted_iota(jnp.int32, (GMLP_BLOCK, GMLP_BLOCK), 1) // 64
    wmask = bcol <= brow
    nblk = L // GMLP_BLOCK

    def rowsum(a):
        return jnp.dot(a.astype(BF16), ones_rhs, preferred_element_type=F32)

    for h in range(HEADS):
        hs = slice(h * HEAD_DIM, (h + 1) * HEAD_DIM)
        li_r = gates_t[h:h + 1, :]
        b_r = bcum_t[HEADS + h:HEADS + h + 1, :]
        b_c = jnp.broadcast_to(bcum[:, HEADS + h:HEADS + h + 1], (L, LANES))
        m_prev = m_scr[h, 0:1, 0:1]
        dmat = jnp.where(causal, (_rep2(b_c) - b_r) + li_r, -jnp.inf)
        m_inter = b_c + m_prev
        m_t = jnp.maximum(m_inter, jnp.max(dmat, axis=-1, keepdims=True))
        wts = jnp.exp(dmat - _rep2(m_t))
        qh = q_ref[h]
        kt = kt_ref[h]
        vh = mv_ref[h]
        s = jnp.dot(qh, kt, preferred_element_type=F32) * wts
        sb = s.astype(BF16)
        inter = jnp.exp(m_inter - m_t)
        c_old = c_scr[h]
        n_old = n_scr[h]
        n_hi = n_old.astype(BF16)
        n_lo = (n_old - n_hi.astype(F32)).astype(BF16)
        qn2 = jnp.dot(qh, jnp.concatenate([n_hi, n_lo], axis=1), preferred_element_type=F32)
        qn = qn2[:, :LANES] + qn2[:, LANES:]
        num = (jnp.dot(sb, vh, preferred_element_type=F32)
               + _rep2(inter) * jnp.dot(qh, c_old.astype(BF16), preferred_element_type=F32))
        den = jnp.dot(sb, ones_rhs, preferred_element_type=F32) + inter * qn
        inv = 1.0 / jnp.maximum(jnp.abs(den), jnp.exp(-m_t))
        hh = num * _rep2(inv)

        b_last = b_r[:, L - 1:L]
        g_r = (b_last - b_r) + li_r
        m_new = jnp.maximum(b_last + m_prev, jnp.max(g_r, axis=-1, keepdims=True))
        decay = jnp.exp((b_last + m_prev) - m_new)
        wkt = (jnp.exp(g_r - m_new) * kt.astype(F32)).astype(BF16)
        c_scr[h] = decay * c_old + jnp.dot(wkt, vh, preferred_element_type=F32)
        n_scr[h] = decay * n_old + jnp.dot(wkt, ones_rhs, preferred_element_type=F32)
        m_scr[h] = jnp.broadcast_to(m_new, (SUBLANES, LANES))

        mu = rowsum(hh) * (1.0 / HEAD_DIM)
        dh = hh - _rep2(mu)
        var = rowsum(dh * dh) * (1.0 / HEAD_DIM)
        y_b = (dh * _rep2(lax.rsqrt(var + EPS))) * hng_ref[:, hs] * o_ref[h].astype(F32)

        for half in range(CW // GROUP_DIM):
            g = h * (CW // GROUP_DIM) + half
            ls = slice(half * GROUP_DIM, (half + 1) * GROUP_DIM)
            wsg = jnp.where(wmask, ws_ref[g], 0.0).astype(BF16)
            vcat = jnp.concatenate(
                [v_ref[h, n * GMLP_BLOCK:(n + 1) * GMLP_BLOCK, ls] for n in range(nblk)], axis=1)
            mixed = jnp.dot(wsg, vcat, preferred_element_type=F32) + _rep2(bsb_ref[g])
            for n in range(nblk):
                rs = slice(n * GMLP_BLOCK, (n + 1) * GMLP_BLOCK)
                y_a = u_ref[h, rs, ls].astype(F32) * mixed[:, n * GMLP_BLOCK:(n + 1) * GMLP_BLOCK]
                merged = (ga_ref[h, rs, ls].astype(F32) * y_a
                          + gb_ref[h, rs, ls].astype(F32) * y_b[rs, ls])
                merged_scr[rs, g * GROUP_DIM:(g + 1) * GROUP_DIM] = merged.astype(BF16)

    proj = jnp.dot(merged_scr[...], wout_ref[...], preferred_element_type=F32)
    h1_ref[...] = x_ref[...] + g1_ref[...] * proj


def _mixer(x2, g1, u, v, q, kt, mv, o, ga, gb, gates, ws, bs_rep, hn_g, w_out):
    nsb = SEQ // L_MIX
    row = lambda b, j: (b * nsb + j, 0)
    seg_in = pl.BlockSpec((NCH, L_MIX, CW), lambda b, j: (0, b * nsb + j, 0))
    kt_in = pl.BlockSpec((NCH, CW, L_MIX), lambda b, j: (0, 0, b * nsb + j))
    return pl.pallas_call(
        _mixer_kernel,
        grid=(BATCH, nsb),
        in_specs=[pl.BlockSpec((L_MIX, D), row),
                  pl.BlockSpec((None, 1, D), lambda b, j: (b, 0, 0)),
                  seg_in, seg_in, seg_in, kt_in, seg_in, seg_in, seg_in, seg_in,
                  pl.BlockSpec((L_MIX, LANES), row),
                  _resident((GMLP_GROUPS, GMLP_BLOCK, GMLP_BLOCK)),
                  _resident((GMLP_GROUPS, GMLP_BLOCK, LANES)),
                  _resident((1, D)),
                  _resident((D, D))],
        out_specs=pl.BlockSpec((L_MIX, D), row),
        out_shape=jax.ShapeDtypeStruct((TOKENS, D), F32),
        scratch_shapes=[pltpu.VMEM((HEADS, HEAD_DIM, HEAD_DIM), F32),
                        pltpu.VMEM((HEADS, HEAD_DIM, LANES), F32),
                        pltpu.VMEM((HEADS, SUBLANES, LANES), F32),
                        pltpu.VMEM((L_MIX, D), BF16)],
        compiler_params=pltpu.CompilerParams(dimension_semantics=("arbitrary", "arbitrary"),
                                             vmem_limit_bytes=VMEM_LIMIT),
        name="mixer",
    )(x2, g1, u, v, q, kt, mv, o, ga, gb, gates, ws, bs_rep, hn_g, w_out)


def _ffn_kernel(h_ref, sh_ref, sc_ref, g2_ref, ng_ref, w1_ref, w2_ref, fg_ref, o_ref):
    h = h_ref[...]
    xn = _rms_mod(h, ng_ref[...], sh_ref[...], sc_ref[...]).astype(BF16)
    acc = jnp.zeros((TM_FFN, D), F32)
    for c in range(FFN // FFN_CHUNK):
        cs = slice(c * FFN_CHUNK, (c + 1) * FFN_CHUNK)
        a = jnp.maximum(jnp.dot(xn, w1_ref[:, cs], preferred_element_type=F32), 0.0)
        acc = acc + jnp.dot((a * a).astype(BF16), w2_ref[cs, :], preferred_element_type=F32)
    h2 = h + g2_ref[...] * acc
    y = h2 * lax.rsqrt(jnp.mean(h2 * h2, axis=-1, keepdims=True) + EPS)
    o_ref[...] = y * fg_ref[...]


def _ffn(h1, sh2, sc2, g2, norm_g, w1, w2, final_g):
    nsb = SEQ // TM_FFN
    row = lambda i: (i, 0)
    per_batch = lambda i: (i // nsb, 0, 0)
    return pl.pallas_call(
        _ffn_kernel,
        grid=(TOKENS // TM_FFN,),
        in_specs=[pl.BlockSpec((TM_FFN, D), row),
                  pl.BlockSpec((None, 1, D), per_batch),
                  pl.BlockSpec((None, 1, D), per_batch),
                  pl.BlockSpec((None, 1, D), per_batch),
                  _resident((1, D)),
                  _resident((D, FFN)),
                  _resident((FFN, D)),
                  _resident((1, D))],
        out_specs=pl.BlockSpec((TM_FFN, D), row),
        out_shape=jax.ShapeDtypeStruct((TOKENS, D), F32),
        compiler_params=pltpu.CompilerParams(dimension_semantics=("arbitrary",),
                                             vmem_limit_bytes=VMEM_LIMIT),
        name="ffn",
    )(h1, sh2, sc2, g2, norm_g, w1, w2, final_g)


def kernel(x, c, w_ada, b_ada, norm1_g, w_in, conv_w, conv_b, mlstm_gate_b, gmlp_ln_g, gmlp_ln_b,
           gmlp_ws, gmlp_bs, mlstm_hn_g, w_out, norm2_g, w_ff1, w_ff2, final_g):
    l = 0
    mod = _ada(c, w_ada[l], b_ada[l]).reshape(BATCH, 6, 1, D)
    sh1, sc1, g1, sh2, sc2, g2 = (mod[:, t] for t in range(6))

    w = w_in[l]
    w_segs = jnp.concatenate([w[:, :OFF_I], w[:, OFF_GA:]], axis=1).astype(BF16)
    w_all = w_segs.reshape(D, N_SEG, NCH, CW).transpose(1, 2, 0, 3)
    w_gate = jnp.pad(w[:, OFF_I:OFF_GA], ((0, 0), (0, LANES - 2 * HEADS))).astype(BF16)
    gate_bias = jnp.pad(mlstm_gate_b[l].reshape(1, 2 * HEADS), ((0, 0), (0, LANES - 2 * HEADS)))
    cw = conv_w[l].reshape(CONV_K, 2, NCH, CW).transpose(1, 2, 0, 3)
    cb = conv_b[l].reshape(2, NCH, 1, CW)

    x2 = x.reshape(TOKENS, D)
    u, v, q, kt, mv, o, ga, gb, gates = _inproj(
        x2, sh1, sc1, norm1_g[l].reshape(1, D), w_all, w_gate, gate_bias, cw, cb,
        gmlp_ln_g[l].reshape(NCH, 1, CW), gmlp_ln_b[l].reshape(NCH, 1, CW))

    bs_rep = jnp.broadcast_to(gmlp_bs[l][:, :, None], (GMLP_GROUPS, GMLP_BLOCK, LANES))
    h1 = _mixer(x2, g1, u, v, q, kt, mv, o, ga, gb, gates,
                gmlp_ws[l], bs_rep, mlstm_hn_g[l].reshape(1, D), w_out[l].astype(BF16))

    out = _ffn(h1, sh2, sc2, g2, norm2_g[l].reshape(1, D),
               w_ff1[l].astype(BF16), w_ff2[l].astype(BF16), final_g.reshape(1, D))
    return out.reshape(BATCH, SEQ, D)
```

```python
import jax
import jax.numpy as jnp
from jax import lax
from jax.experimental import pallas as pl
from jax.experimental.pallas import tpu as pltpu

D = 1024
BATCH = 16
SEQ = 2048
TOKENS = BATCH * SEQ
GMLP_BLOCK = 128
GMLP_GROUPS = 8
GROUP_DIM = D // GMLP_GROUPS
HEADS = 4
HEAD_DIM = 256
CONV_K = 4
FFN = 4 * D
EPS = 1e-6
LANES = 128
SUBLANES = 8

OFF_I = 6 * D
OFF_GA = OFF_I + 2 * HEADS
N_SEG = 8
SEG_U, SEG_V, SEG_Q, SEG_K, SEG_MV, SEG_O, SEG_GA, SEG_GB = range(N_SEG)
CW = 256
NCH = D // CW

TM_IN = 512
ROWS_PIECE = 512
L_MIX = 256
TM_FFN = 512
FFN_CHUNK = 1024
ADA_COLS = 1536

F32 = jnp.float32
BF16 = jnp.bfloat16
VMEM_LIMIT = 56 * 1024 * 1024


def _resident(shape):
    nd = len(shape)
    return pl.BlockSpec(shape, lambda *_: (0,) * nd, pipeline_mode=pl.Buffered(1))


def _ada_kernel(c_ref, w_ref, b_ref, o_ref):
    c = c_ref[...]
    ca = c * jax.nn.sigmoid(c)
    o_ref[...] = jnp.dot(ca, w_ref[...], preferred_element_type=F32,
                         precision=lax.Precision.HIGHEST) + b_ref[...]


def _ada(c, w, b):
    n = w.shape[1]
    return pl.pallas_call(
        _ada_kernel,
        grid=(n // ADA_COLS,),
        in_specs=[pl.BlockSpec((BATCH, D), lambda j: (0, 0)),
                  pl.BlockSpec((D, ADA_COLS), lambda j: (0, j)),
                  pl.BlockSpec((1, ADA_COLS), lambda j: (0, j))],
        out_specs=pl.BlockSpec((BATCH, ADA_COLS), lambda j: (0, j)),
        out_shape=jax.ShapeDtypeStruct((BATCH, n), F32),
        compiler_params=pltpu.CompilerParams(vmem_limit_bytes=VMEM_LIMIT),
        name="ada",
    )(c, w, b.reshape(1, n))


def _wprep_kernel(a_ref, b_ref, o_ref):
    s = pl.program_id(0)

    @pl.when(s < SEG_GA)
    def _():
        o_ref[...] = a_ref[...].astype(BF16)

    @pl.when(s >= SEG_GA)
    def _():
        keep = CW - 2 * HEADS
        lane = lax.broadcasted_iota(jnp.int32, (D, CW), 1)
        o_ref[...] = jnp.where(lane < keep, pltpu.roll(a_ref[...], keep, axis=1),
                               pltpu.roll(b_ref[...], keep, axis=1)).astype(BF16)


def _wprep(w):
    last = (w.shape[1] - 1) // CW
    return pl.pallas_call(
        _wprep_kernel,
        grid=(N_SEG, NCH),
        in_specs=[pl.BlockSpec((D, CW), lambda s, c: (0, s * NCH + c)),
                  pl.BlockSpec((D, CW), lambda s, c: (0, jnp.minimum(s * NCH + c + 1, last)))],
        out_specs=pl.BlockSpec((None, None, D, CW), lambda s, c: (s, c, 0, 0)),
        out_shape=jax.ShapeDtypeStruct((N_SEG, NCH, D, CW), BF16),
        compiler_params=pltpu.CompilerParams(vmem_limit_bytes=VMEM_LIMIT),
        name="wprep",
    )(w, w)


def _rms_mod(x, g, shift, scale):
    y = x * lax.rsqrt(jnp.mean(x * x, axis=-1, keepdims=True) + EPS)
    return (y * g) * (1.0 + scale) + shift


def _conv_silu(z, halo_ref, c, w, b):
    tm = z.shape[0]
    zz = jnp.concatenate([halo_ref[c], z], axis=0)
    acc = b + w[CONV_K - 1:CONV_K, :] * z
    for j in range(CONV_K - 1):
        shifted = pltpu.roll(zz, CONV_K - 1 - j, axis=0)[SUBLANES:, :]
        acc = acc + w[j:j + 1, :] * shifted
    halo_ref[c] = z[tm - SUBLANES:, :]
    return acc * jax.nn.sigmoid(acc)


def _inproj_kernel(x_ref, sh_ref, sc_ref, g_ref, w_ref, wg_ref, gb_ref, cw_ref, cb_ref,
                   lng_ref, lnb_ref,
                   u_ref, v_ref, q_ref, kt_ref, mv_ref, o_ref, ga_ref, gbo_ref, gate_ref,
                   xn_scr, gv_scr, s1_scr, halo_scr):
    i = pl.program_id(0)

    @pl.when((i % (SEQ // TM_IN)) == 0)
    def _():
        halo_scr[...] = jnp.zeros(halo_scr.shape, F32)

    xn_scr[...] = _rms_mod(x_ref[...], g_ref[...], sh_ref[...], sc_ref[...]).astype(BF16)
    s1_scr[...] = jnp.zeros(s1_scr.shape, F32)

    def seg(k, c, rs):
        return jnp.dot(xn_scr[rs, :], w_ref[k, c], preferred_element_type=F32)

    pieces = [slice(p * ROWS_PIECE, (p + 1) * ROWS_PIECE) for p in range(TM_IN // ROWS_PIECE)]

    def loop_main(c, carry):
        for rs in pieces:
            k = _conv_silu(seg(SEG_K, c, rs), halo_scr.at[1], c, cw_ref[1, c], cb_ref[1, c])
            kt_ref[c, :, rs] = k.T.astype(BF16)
            gv = jax.nn.gelu(seg(SEG_V, c, rs))
            gv_scr[c, rs, :] = gv
            s1_scr[rs, :] += gv[:, :LANES] + gv[:, LANES:]
            q = _conv_silu(seg(SEG_Q, c, rs), halo_scr.at[0], c, cw_ref[0, c], cb_ref[0, c])
            q_ref[c, rs, :] = (q * (HEAD_DIM ** -0.5)).astype(BF16)
            u_ref[c, rs, :] = jax.nn.gelu(seg(SEG_U, c, rs)).astype(BF16)
            o_ref[c, rs, :] = jax.nn.sigmoid(seg(SEG_O, c, rs)).astype(BF16)
            ga_ref[c, rs, :] = jax.nn.sigmoid(seg(SEG_GA, c, rs)).astype(BF16)
            gbo_ref[c, rs, :] = jax.nn.sigmoid(seg(SEG_GB, c, rs)).astype(BF16)
            mv_ref[c, rs, :] = seg(SEG_MV, c, rs).astype(BF16)
        return carry

    lax.fori_loop(0, NCH, loop_main, 0)

    gp = jnp.dot(xn_scr[...], wg_ref[...], preferred_element_type=F32) + gb_ref[...]
    lane = lax.broadcasted_iota(jnp.int32, gp.shape, 1)
    gate_ref[...] = jnp.where(lane < HEADS, gp, jax.nn.log_sigmoid(gp))

    mu = jnp.sum(s1_scr[...], axis=-1, keepdims=True) * (1.0 / D)
    ssq = jnp.zeros((TM_IN, LANES), F32)
    for c in range(NCH):
        dv = gv_scr[c] - mu
        dv = dv * dv
        ssq = ssq + (dv[:, :LANES] + dv[:, LANES:])
    rstd = lax.rsqrt(jnp.sum(ssq, axis=-1, keepdims=True) * (1.0 / D) + EPS)
    for c in range(NCH):
        v_ref[c] = (((gv_scr[c] - mu) * rstd) * lng_ref[c] + lnb_ref[c]).astype(BF16)


def _inproj(x2, sh1, sc1, norm_g, w_all, w_gate, gate_bias, conv_w, conv_b, ln_g, ln_b):
    nsb = SEQ // TM_IN
    row = lambda i: (i, 0)
    per_batch = lambda i: (i // nsb, 0, 0)
    seg_out = pl.BlockSpec((NCH, TM_IN, CW), lambda i: (0, i, 0))
    seg_shape = jax.ShapeDtypeStruct((NCH, TOKENS, CW), BF16)
    seg_outs = [seg_out] * N_SEG
    seg_shapes = [seg_shape] * N_SEG
    seg_outs[SEG_K] = pl.BlockSpec((NCH, CW, TM_IN), lambda i: (0, 0, i))
    seg_shapes[SEG_K] = jax.ShapeDtypeStruct((NCH, CW, TOKENS), BF16)
    return pl.pallas_call(
        _inproj_kernel,
        grid=(TOKENS // TM_IN,),
        in_specs=[pl.BlockSpec((TM_IN, D), row),
                  pl.BlockSpec((None, 1, D), per_batch),
                  pl.BlockSpec((None, 1, D), per_batch),
                  _resident((1, D)),
                  _resident((N_SEG, NCH, D, CW)),
                  _resident((D, LANES)),
                  _resident((1, LANES)),
                  _resident((2, NCH, CONV_K, CW)),
                  _resident((2, NCH, 1, CW)),
                  _resident((NCH, 1, CW)),
                  _resident((NCH, 1, CW))],
        out_specs=seg_outs + [pl.BlockSpec((TM_IN, LANES), row)],
        out_shape=seg_shapes + [jax.ShapeDtypeStruct((TOKENS, LANES), F32)],
        scratch_shapes=[pltpu.VMEM((TM_IN, D), BF16),
                        pltpu.VMEM((NCH, TM_IN, CW), F32),
                        pltpu.VMEM((TM_IN, LANES), F32),
                        pltpu.VMEM((2, NCH, SUBLANES, CW), F32)],
        compiler_params=pltpu.CompilerParams(dimension_semantics=("arbitrary",),
                                             vmem_limit_bytes=VMEM_LIMIT),
        name="inproj",
    )(x2, sh1, sc1, norm_g, w_all, w_gate, gate_bias, conv_w, conv_b, ln_g, ln_b)


def _cumsum_rows(x):
    n = x.shape[0]
    row = lax.broadcasted_iota(jnp.int32, x.shape, 0)
    k = 1
    while k < n:
        x = x + jnp.where(row >= k, pltpu.roll(x, k, axis=0), 0.0)
        k *= 2
    return x


def _rep2(a):
    return jnp.concatenate([a, a], axis=1)


def _mixer_kernel(x_ref, g1_ref, u_ref, v_ref, q_ref, kt_ref, mv_ref, o_ref, ga_ref, gb_ref,
                  gate_ref, ws_ref, bsb_ref, hng_ref, wout_ref,
                  h1_ref,
                  c_scr, n_scr, m_scr, merged_scr):
    L = L_MIX
    j = pl.program_id(1)

    @pl.when(j == 0)
    def _():
        c_scr[...] = jnp.zeros(c_scr.shape, F32)
        n_scr[...] = jnp.zeros(n_scr.shape, F32)
        m_scr[...] = jnp.zeros(m_scr.shape, F32)

    gates = gate_ref[...]
    bcum = _cumsum_rows(gates)
    gates_t = gates.T
    bcum_t = bcum.T
    row = lax.broadcasted_iota(jnp.int32, (L, L), 0)
    col = lax.broadcasted_iota(jnp.int32, (L, L), 1)
    causal = col <= row
    ones_rhs = jnp.ones((L, LANES), BF16)

    brow = lax.broadcasted_iota(jnp.int32, (GMLP_BLOCK, GMLP_BLOCK), 0) // 64
    bcol = lax.broadcasted_iota(jnp.int32, (GMLP_BLOCK, GMLP_BLOCK), 1) // 64
    wmask = bcol <= brow
    nblk = L // GMLP_BLOCK

    for h in range(HEADS):
        hs = slice(h * HEAD_DIM, (h + 1) * HEAD_DIM)
        li_r = gates_t[h:h + 1, :]
        b_r = bcum_t[HEADS + h:HEADS + h + 1, :]
        b_c = jnp.broadcast_to(bcum[:, HEADS + h:HEADS + h + 1], (L, LANES))
        m_prev = m_scr[h, 0:1, 0:1]
        dmat = jnp.where(causal, _rep2(b_c) + (li_r - b_r), -jnp.inf)
        m_inter = b_c + m_prev
        m_t = jnp.maximum(m_inter, jnp.max(dmat, axis=-1, keepdims=True))
        wts = jnp.exp(dmat - _rep2(m_t))
        qh = q_ref[h]
        kt = kt_ref[h]
        vh = mv_ref[h]
        s = jnp.dot(qh, kt, preferred_element_type=F32) * wts
        sb = s.astype(BF16)
        inter = jnp.exp(m_inter - m_t)
        c_old = c_scr[h]
        n_old = n_scr[h]
        n_hi = n_old.astype(BF16)
        n_lo = (n_old - n_hi.astype(F32)).astype(BF16)
        qn2 = jnp.dot(qh, jnp.concatenate([n_hi, n_lo], axis=1), preferred_element_type=F32)
        qn = qn2[:, :LANES] + qn2[:, LANES:]
        num = (jnp.dot(sb, vh, preferred_element_type=F32)
               + _rep2(inter) * jnp.dot(qh, c_old.astype(BF16), preferred_element_type=F32))
        den = jnp.dot(sb, ones_rhs, preferred_element_type=F32) + inter * qn
        inv = 1.0 / jnp.maximum(jnp.abs(den), jnp.exp(-m_t))
        hh = num * _rep2(inv)

        b_last = b_r[:, L - 1:L]
        g_r = (b_last - b_r) + li_r
        m_new = jnp.maximum(b_last + m_prev, jnp.max(g_r, axis=-1, keepdims=True))
        decay = jnp.exp((b_last + m_prev) - m_new)
        wkt = jnp.exp(g_r - m_new).astype(BF16) * kt
        c_scr[h] = decay * c_old + jnp.dot(wkt, vh, preferred_element_type=F32)
        n_scr[h] = decay * n_old + jnp.dot(wkt, ones_rhs, preferred_element_type=F32)
        m_scr[h] = jnp.broadcast_to(m_new, (SUBLANES, LANES))

        mu = jnp.mean(hh, axis=-1, keepdims=True)
        dh = hh - mu
        var = jnp.mean(dh * dh, axis=-1, keepdims=True)
        y_n = (dh * lax.rsqrt(var + EPS)) * hng_ref[:, hs]

        for half in range(CW // GROUP_DIM):
            g = h * (CW // GROUP_DIM) + half
            ls = slice(half * GROUP_DIM, (half + 1) * GROUP_DIM)
            wsg = jnp.where(wmask, ws_ref[g], 0.0).astype(BF16)
            vcat = jnp.concatenate(
                [v_ref[h, n * GMLP_BLOCK:(n + 1) * GMLP_BLOCK, ls] for n in range(nblk)], axis=1)
            mixed = jnp.dot(wsg, vcat, preferred_element_type=F32) + _rep2(bsb_ref[g])
            for n in range(nblk):
                rs = slice(n * GMLP_BLOCK, (n + 1) * GMLP_BLOCK)
                gate_a = (ga_ref[h, rs, ls] * u_ref[h, rs, ls]).astype(F32)
                gate_b = (gb_ref[h, rs, ls] * o_ref[h, rs, ls]).astype(F32)
                merged = (gate_a * mixed[:, n * GMLP_BLOCK:(n + 1) * GMLP_BLOCK]
                          + gate_b * y_n[rs, ls])
                merged_scr[rs, g * GROUP_DIM:(g + 1) * GROUP_DIM] = merged.astype(BF16)

    proj = jnp.dot(merged_scr[...], wout_ref[...], preferred_element_type=F32)
    h1_ref[...] = x_ref[...] + g1_ref[...] * proj


def _mixer(x2, g1, u, v, q, kt, mv, o, ga, gb, gates, ws, bs_rep, hn_g, w_out):
    nsb = SEQ // L_MIX
    row = lambda b, j: (b * nsb + j, 0)
    seg_in = pl.BlockSpec((NCH, L_MIX, CW), lambda b, j: (0, b * nsb + j, 0))
    kt_in = pl.BlockSpec((NCH, CW, L_MIX), lambda b, j: (0, 0, b * nsb + j))
    return pl.pallas_call(
        _mixer_kernel,
        grid=(BATCH, nsb),
        in_specs=[pl.BlockSpec((L_MIX, D), row),
                  pl.BlockSpec((None, 1, D), lambda b, j: (b, 0, 0)),
                  seg_in, seg_in, seg_in, kt_in, seg_in, seg_in, seg_in, seg_in,
                  pl.BlockSpec((L_MIX, LANES), row),
                  _resident((GMLP_GROUPS, GMLP_BLOCK, GMLP_BLOCK)),
                  _resident((GMLP_GROUPS, GMLP_BLOCK, LANES)),
                  _resident((1, D)),
                  _resident((D, D))],
        out_specs=pl.BlockSpec((L_MIX, D), row),
        out_shape=jax.ShapeDtypeStruct((TOKENS, D), F32),
        scratch_shapes=[pltpu.VMEM((HEADS, HEAD_DIM, HEAD_DIM), F32),
                        pltpu.VMEM((HEADS, HEAD_DIM, LANES), F32),
                        pltpu.VMEM((HEADS, SUBLANES, LANES), F32),
                        pltpu.VMEM((L_MIX, D), BF16)],
        compiler_params=pltpu.CompilerParams(dimension_semantics=("arbitrary", "arbitrary"),
                                             vmem_limit_bytes=VMEM_LIMIT),
        name="mixer",
    )(x2, g1, u, v, q, kt, mv, o, ga, gb, gates, ws, bs_rep, hn_g, w_out)


def _ffn_kernel(h_ref, sh_ref, sc_ref, g2_ref, ng_ref, w1_ref, w2_ref, fg_ref, o_ref):
    h = h_ref[...]
    xn = _rms_mod(h, ng_ref[...], sh_ref[...], sc_ref[...]).astype(BF16)
    acc = jnp.zeros((TM_FFN, D), F32)
    for c in range(FFN // FFN_CHUNK):
        cs = slice(c * FFN_CHUNK, (c + 1) * FFN_CHUNK)
        a = jnp.maximum(jnp.dot(xn, w1_ref[:, cs], preferred_element_type=F32), 0.0)
        acc = acc + jnp.dot((a * a).astype(BF16), w2_ref[cs, :], preferred_element_type=F32)
    h2 = h + g2_ref[...] * acc
    y = h2 * lax.rsqrt(jnp.mean(h2 * h2, axis=-1, keepdims=True) + EPS)
    o_ref[...] = y * fg_ref[...]


def _ffn(h1, sh2, sc2, g2, norm_g, w1, w2, final_g):
    nsb = SEQ // TM_FFN
    row = lambda i: (i, 0)
    per_batch = lambda i: (i // nsb, 0, 0)
    return pl.pallas_call(
        _ffn_kernel,
        grid=(TOKENS // TM_FFN,),
        in_specs=[pl.BlockSpec((TM_FFN, D), row),
                  pl.BlockSpec((None, 1, D), per_batch),
                  pl.BlockSpec((None, 1, D), per_batch),
                  pl.BlockSpec((None, 1, D), per_batch),
                  _resident((1, D)),
                  _resident((D, FFN)),
                  _resident((FFN, D)),
                  _resident((1, D))],
        out_specs=pl.BlockSpec((TM_FFN, D), row),
        out_shape=jax.ShapeDtypeStruct((TOKENS, D), F32),
        compiler_params=pltpu.CompilerParams(dimension_semantics=("arbitrary",),
                                             vmem_limit_bytes=VMEM_LIMIT),
        name="ffn",
    )(h1, sh2, sc2, g2, norm_g, w1, w2, final_g)


def kernel(x, c, w_ada, b_ada, norm1_g, w_in, conv_w, conv_b, mlstm_gate_b, gmlp_ln_g, gmlp_ln_b,
           gmlp_ws, gmlp_bs, mlstm_hn_g, w_out, norm2_g, w_ff1, w_ff2, final_g):
    l = 0
    mod = _ada(c, w_ada[l], b_ada[l]).reshape(BATCH, 6, 1, D)
    sh1, sc1, g1, sh2, sc2, g2 = (mod[:, t] for t in range(6))

    w = w_in[l]
    w_all = _wprep(w)
    w_gate = jnp.pad(w[:, OFF_I:OFF_GA], ((0, 0), (0, LANES - 2 * HEADS))).astype(BF16)
    gate_bias = jnp.pad(mlstm_gate_b[l].reshape(1, 2 * HEADS), ((0, 0), (0, LANES - 2 * HEADS)))
    cw = conv_w[l].reshape(CONV_K, 2, NCH, CW).transpose(1, 2, 0, 3)
    cb = conv_b[l].reshape(2, NCH, 1, CW)

    x2 = x.reshape(TOKENS, D)
    u, v, q, kt, mv, o, ga, gb, gates = _inproj(
        x2, sh1, sc1, norm1_g[l].reshape(1, D), w_all, w_gate, gate_bias, cw, cb,
        gmlp_ln_g[l].reshape(NCH, 1, CW), gmlp_ln_b[l].reshape(NCH, 1, CW))

    bs_rep = jnp.broadcast_to(gmlp_bs[l][:, :, None], (GMLP_GROUPS, GMLP_BLOCK, LANES))
    h1 = _mixer(x2, g1, u, v, q, kt, mv, o, ga, gb, gates,
                gmlp_ws[l], bs_rep, mlstm_hn_g[l].reshape(1, D), w_out[l].astype(BF16))

    out = _ffn(h1, sh2, sc2, g2, norm2_g[l].reshape(1, D),
               w_ff1[l].astype(BF16), w_ff2[l].astype(BF16), final_g.reshape(1, D))
    return out.reshape(BATCH, SEQ, D)
```

```python
import jax
import jax.numpy as jnp
from jax import lax
from jax.experimental import pallas as pl
from jax.experimental.pallas import tpu as pltpu

D = 1024
BATCH = 16
SEQ = 2048
TOKENS = BATCH * SEQ
GMLP_BLOCK = 128
GMLP_GROUPS = 8
GROUP_DIM = D // GMLP_GROUPS
HEADS = 4
HEAD_DIM = 256
CONV_K = 4
FFN = 4 * D
EPS = 1e-6
LANES = 128
SUBLANES = 8

OFF_I = 6 * D
OFF_GA = OFF_I + 2 * HEADS
N_SEG = 8
SEG_U, SEG_V, SEG_Q, SEG_K, SEG_MV, SEG_O, SEG_GA, SEG_GB = range(N_SEG)
CW = 256
NCH = D // CW

TM_IN = 512
ROWS_PIECE = 512
CONV_PIECE = 256
L_MIX = 256
TM_FFN = 512
FFN_CHUNK = 1024
ADA_COLS = 1536

F32 = jnp.float32
BF16 = jnp.bfloat16
VMEM_LIMIT = 56 * 1024 * 1024


def _resident(shape):
    nd = len(shape)
    return pl.BlockSpec(shape, lambda *_: (0,) * nd, pipeline_mode=pl.Buffered(1))


def _ada_kernel(c_ref, w_ref, b_ref, o_ref):
    c = c_ref[...]
    ca = c * jax.nn.sigmoid(c)
    o_ref[...] = jnp.dot(ca, w_ref[...], preferred_element_type=F32,
                         precision=lax.Precision.HIGHEST) + b_ref[...]


def _ada(c, w, b):
    n = w.shape[1]
    return pl.pallas_call(
        _ada_kernel,
        grid=(n // ADA_COLS,),
        in_specs=[pl.BlockSpec((BATCH, D), lambda j: (0, 0)),
                  pl.BlockSpec((D, ADA_COLS), lambda j: (0, j)),
                  pl.BlockSpec((1, ADA_COLS), lambda j: (0, j))],
        out_specs=pl.BlockSpec((BATCH, ADA_COLS), lambda j: (0, j)),
        out_shape=jax.ShapeDtypeStruct((BATCH, n), F32),
        compiler_params=pltpu.CompilerParams(vmem_limit_bytes=VMEM_LIMIT),
        name="ada",
    )(c, w, b.reshape(1, n))


def _wprep_kernel(a_ref, b_ref, o_ref):
    s = pl.program_id(0)

    @pl.when(s < SEG_GA)
    def _():
        o_ref[...] = a_ref[...].astype(BF16)

    @pl.when(s >= SEG_GA)
    def _():
        keep = CW - 2 * HEADS
        lane = lax.broadcasted_iota(jnp.int32, (D, CW), 1)
        o_ref[...] = jnp.where(lane < keep, pltpu.roll(a_ref[...], keep, axis=1),
                               pltpu.roll(b_ref[...], keep, axis=1)).astype(BF16)


def _wprep(w):
    last = (w.shape[1] - 1) // CW
    return pl.pallas_call(
        _wprep_kernel,
        grid=(N_SEG, NCH),
        in_specs=[pl.BlockSpec((D, CW), lambda s, c: (0, s * NCH + c)),
                  pl.BlockSpec((D, CW), lambda s, c: (0, jnp.minimum(s * NCH + c + 1, last)))],
        out_specs=pl.BlockSpec((None, None, D, CW), lambda s, c: (s, c, 0, 0)),
        out_shape=jax.ShapeDtypeStruct((N_SEG, NCH, D, CW), BF16),
        compiler_params=pltpu.CompilerParams(vmem_limit_bytes=VMEM_LIMIT),
        name="wprep",
    )(w, w)


def _rms_mod(x, g, shift, scale):
    y = x * lax.rsqrt(jnp.mean(x * x, axis=-1, keepdims=True) + EPS)
    return (y * g) * (1.0 + scale) + shift


def _conv_silu(z, halo_ref, c, w, b):
    tm = z.shape[0]
    zz = jnp.concatenate([halo_ref[c], z], axis=0)
    acc = b + w[CONV_K - 1:CONV_K, :] * z
    for j in range(CONV_K - 1):
        shifted = pltpu.roll(zz, CONV_K - 1 - j, axis=0)[SUBLANES:, :]
        acc = acc + w[j:j + 1, :] * shifted
    halo_ref[c] = z[tm - SUBLANES:, :]
    return acc * jax.nn.sigmoid(acc)


def _inproj_kernel(x_ref, sh_ref, sc_ref, g_ref, w_ref, wg_ref, gb_ref, cw_ref, cb_ref,
                   lng_ref, lnb_ref,
                   u_ref, v_ref, q_ref, kt_ref, mv_ref, o_ref, ga_ref, gbo_ref, gate_ref,
                   xn_scr, gv_scr, s1_scr, halo_scr):
    i = pl.program_id(0)

    @pl.when((i % (SEQ // TM_IN)) == 0)
    def _():
        halo_scr[...] = jnp.zeros(halo_scr.shape, F32)

    xn_scr[...] = _rms_mod(x_ref[...], g_ref[...], sh_ref[...], sc_ref[...]).astype(BF16)
    s1_scr[...] = jnp.zeros(s1_scr.shape, F32)

    def seg(k, c, rs):
        return jnp.dot(xn_scr[rs, :], w_ref[k, c], preferred_element_type=F32)

    pieces = [slice(p * ROWS_PIECE, (p + 1) * ROWS_PIECE) for p in range(TM_IN // ROWS_PIECE)]
    conv_pieces = [slice(p * CONV_PIECE, (p + 1) * CONV_PIECE) for p in range(TM_IN // CONV_PIECE)]

    def loop_main(c, carry):
        for rs in pieces:
            for hs in conv_pieces:
                k = _conv_silu(seg(SEG_K, c, hs), halo_scr.at[1], c, cw_ref[1, c], cb_ref[1, c])
                kt_ref[c, :, hs] = k.T.astype(BF16)
            gv = jax.nn.gelu(seg(SEG_V, c, rs))
            gv_scr[c, rs, :] = gv
            s1_scr[rs, :] += gv[:, :LANES] + gv[:, LANES:]
            for hs in conv_pieces:
                q = _conv_silu(seg(SEG_Q, c, hs), halo_scr.at[0], c, cw_ref[0, c], cb_ref[0, c])
                q_ref[c, hs, :] = (q * (HEAD_DIM ** -0.5)).astype(BF16)
            u_ref[c, rs, :] = jax.nn.gelu(seg(SEG_U, c, rs)).astype(BF16)
            o_ref[c, rs, :] = jax.nn.sigmoid(seg(SEG_O, c, rs)).astype(BF16)
            ga_ref[c, rs, :] = jax.nn.sigmoid(seg(SEG_GA, c, rs)).astype(BF16)
            gbo_ref[c, rs, :] = jax.nn.sigmoid(seg(SEG_GB, c, rs)).astype(BF16)
            mv_ref[c, rs, :] = seg(SEG_MV, c, rs).astype(BF16)
        return carry

    lax.fori_loop(0, NCH, loop_main, 0)

    gp = jnp.dot(xn_scr[...], wg_ref[...], preferred_element_type=F32) + gb_ref[...]
    lane = lax.broadcasted_iota(jnp.int32, gp.shape, 1)
    gate_ref[...] = jnp.where(lane < HEADS, gp, jax.nn.log_sigmoid(gp))

    mu = jnp.sum(s1_scr[...], axis=-1, keepdims=True) * (1.0 / D)
    ssq = jnp.zeros((TM_IN, LANES), F32)
    for c in range(NCH):
        dv = gv_scr[c] - mu
        dv = dv * dv
        ssq = ssq + (dv[:, :LANES] + dv[:, LANES:])
    rstd = lax.rsqrt(jnp.sum(ssq, axis=-1, keepdims=True) * (1.0 / D) + EPS)
    for c in range(NCH):
        v_ref[c] = (((gv_scr[c] - mu) * rstd) * lng_ref[c] + lnb_ref[c]).astype(BF16)


def _inproj(x2, sh1, sc1, norm_g, w_all, w_gate, gate_bias, conv_w, conv_b, ln_g, ln_b):
    nsb = SEQ // TM_IN
    row = lambda i: (i, 0)
    per_batch = lambda i: (i // nsb, 0, 0)
    seg_out = pl.BlockSpec((NCH, TM_IN, CW), lambda i: (0, i, 0))
    seg_shape = jax.ShapeDtypeStruct((NCH, TOKENS, CW), BF16)
    seg_outs = [seg_out] * N_SEG
    seg_shapes = [seg_shape] * N_SEG
    seg_outs[SEG_K] = pl.BlockSpec((NCH, CW, TM_IN), lambda i: (0, 0, i))
    seg_shapes[SEG_K] = jax.ShapeDtypeStruct((NCH, CW, TOKENS), BF16)
    return pl.pallas_call(
        _inproj_kernel,
        grid=(TOKENS // TM_IN,),
        in_specs=[pl.BlockSpec((TM_IN, D), row),
                  pl.BlockSpec((None, 1, D), per_batch),
                  pl.BlockSpec((None, 1, D), per_batch),
                  _resident((1, D)),
                  _resident((N_SEG, NCH, D, CW)),
                  _resident((D, LANES)),
                  _resident((1, LANES)),
                  _resident((2, NCH, CONV_K, CW)),
                  _resident((2, NCH, 1, CW)),
                  _resident((NCH, 1, CW)),
                  _resident((NCH, 1, CW))],
        out_specs=seg_outs + [pl.BlockSpec((TM_IN, LANES), row)],
        out_shape=seg_shapes + [jax.ShapeDtypeStruct((TOKENS, LANES), F32)],
        scratch_shapes=[pltpu.VMEM((TM_IN, D), BF16),
                        pltpu.VMEM((NCH, TM_IN, CW), F32),
                        pltpu.VMEM((TM_IN, LANES), F32),
                        pltpu.VMEM((2, NCH, SUBLANES, CW), F32)],
        compiler_params=pltpu.CompilerParams(dimension_semantics=("arbitrary",),
                                             vmem_limit_bytes=VMEM_LIMIT),
        name="inproj",
    )(x2, sh1, sc1, norm_g, w_all, w_gate, gate_bias, conv_w, conv_b, ln_g, ln_b)


def _cumsum_rows(x):
    n = x.shape[0]
    row = lax.broadcasted_iota(jnp.int32, x.shape, 0)
    k = 1
    while k < n:
        x = x + jnp.where(row >= k, pltpu.roll(x, k, axis=0), 0.0)
        k *= 2
    return x


def _rep2(a):
    return jnp.concatenate([a, a], axis=1)


def _mixer_kernel(x_ref, g1_ref, u_ref, v_ref, q_ref, kt_ref, mv_ref, o_ref, ga_ref, gb_ref,
                  gate_ref, ws_ref, bsb_ref, hng_ref, wout_ref,
                  h1_ref,
                  c_scr, n_scr, m_scr, merged_scr):
    L = L_MIX
    j = pl.program_id(1)

    @pl.when(j == 0)
    def _():
        c_scr[...] = jnp.zeros(c_scr.shape, F32)
        n_scr[...] = jnp.zeros(n_scr.shape, F32)
        m_scr[...] = jnp.zeros(m_scr.shape, F32)

    gates = gate_ref[...]
    bcum = _cumsum_rows(gates)
    gates_t = gates.T
    bcum_t = bcum.T
    row = lax.broadcasted_iota(jnp.int32, (L, L), 0)
    col = lax.broadcasted_iota(jnp.int32, (L, L), 1)
    causal = col <= row
    ones_rhs = jnp.ones((L, LANES), BF16)

    brow = lax.broadcasted_iota(jnp.int32, (GMLP_BLOCK, GMLP_BLOCK), 0) // 64
    bcol = lax.broadcasted_iota(jnp.int32, (GMLP_BLOCK, GMLP_BLOCK), 1) // 64
    wmask = bcol <= brow
    nblk = L // GMLP_BLOCK

    for h in range(HEADS):
        hs = slice(h * HEAD_DIM, (h + 1) * HEAD_DIM)
        li_r = gates_t[h:h + 1, :]
        b_r = bcum_t[HEADS + h:HEADS + h + 1, :]
        b_c = jnp.broadcast_to(bcum[:, HEADS + h:HEADS + h + 1], (L, LANES))
        m_prev = m_scr[h, 0:1, 0:1]
        dmat = jnp.where(causal, _rep2(b_c) + (li_r - b_r), -jnp.inf)
        m_inter = b_c + m_prev
        m_t = jnp.maximum(m_inter, jnp.max(dmat, axis=-1, keepdims=True))
        wts = jnp.exp(dmat - _rep2(m_t))
        qh = q_ref[h]
        kt = kt_ref[h]
        vh = mv_ref[h]
        s = jnp.dot(qh, kt, preferred_element_type=F32) * wts
        sb = s.astype(BF16)
        inter = jnp.exp(m_inter - m_t)
        c_old = c_scr[h]
        n_old = n_scr[h]
        n_hi = n_old.astype(BF16)
        n_lo = (n_old - n_hi.astype(F32)).astype(BF16)
        qn2 = jnp.dot(qh, jnp.concatenate([n_hi, n_lo], axis=1), preferred_element_type=F32)
        qn = qn2[:, :LANES] + qn2[:, LANES:]
        num = (jnp.dot(sb, vh, preferred_element_type=F32)
               + _rep2(inter) * jnp.dot(qh, c_old.astype(BF16), preferred_element_type=F32))
        den = jnp.dot(sb, ones_rhs, preferred_element_type=F32) + inter * qn
        inv = 1.0 / jnp.maximum(jnp.abs(den), jnp.exp(-m_t))
        hh = num * _rep2(inv)

        b_last = b_r[:, L - 1:L]
        g_r = (b_last - b_r) + li_r
        m_new = jnp.maximum(b_last + m_prev, jnp.max(g_r, axis=-1, keepdims=True))
        decay = jnp.exp((b_last + m_prev) - m_new)
        wkt = jnp.exp(g_r - m_new).astype(BF16) * kt
        c_scr[h] = decay * c_old + jnp.dot(wkt, vh, preferred_element_type=F32)
        n_scr[h] = decay * n_old + jnp.dot(wkt, ones_rhs, preferred_element_type=F32)
        m_scr[h] = jnp.broadcast_to(m_new, (SUBLANES, LANES))

        mu = jnp.mean(hh, axis=-1, keepdims=True)
        dh = hh - mu
        var = jnp.mean(dh * dh, axis=-1, keepdims=True)
        y_n = (dh * lax.rsqrt(var + EPS)) * hng_ref[:, hs]

        for half in range(CW // GROUP_DIM):
            g = h * (CW // GROUP_DIM) + half
            ls = slice(half * GROUP_DIM, (half + 1) * GROUP_DIM)
            wsg = jnp.where(wmask, ws_ref[g], 0.0).astype(BF16)
            vcat = jnp.concatenate(
                [v_ref[h, n * GMLP_BLOCK:(n + 1) * GMLP_BLOCK, ls] for n in range(nblk)], axis=1)
            mixed = jnp.dot(wsg, vcat, preferred_element_type=F32) + _rep2(bsb_ref[g])
            for n in range(nblk):
                rs = slice(n * GMLP_BLOCK, (n + 1) * GMLP_BLOCK)
                gate_a = (ga_ref[h, rs, ls] * u_ref[h, rs, ls]).astype(F32)
                gate_b = (gb_ref[h, rs, ls] * o_ref[h, rs, ls]).astype(F32)
                merged = (gate_a * mixed[:, n * GMLP_BLOCK:(n + 1) * GMLP_BLOCK]
                          + gate_b * y_n[rs, ls])
                merged_scr[rs, g * GROUP_DIM:(g + 1) * GROUP_DIM] = merged.astype(BF16)

    proj = jnp.dot(merged_scr[...], wout_ref[...], preferred_element_type=F32)
    h1_ref[...] = x_ref[...] + g1_ref[...] * proj


def _mixer(x2, g1, u, v, q, kt, mv, o, ga, gb, gates, ws, bs_rep, hn_g, w_out):
    nsb = SEQ // L_MIX
    row = lambda b, j: (b * nsb + j, 0)
    seg_in = pl.BlockSpec((NCH, L_MIX, CW), lambda b, j: (0, b * nsb + j, 0))
    kt_in = pl.BlockSpec((NCH, CW, L_MIX), lambda b, j: (0, 0, b * nsb + j))
    return pl.pallas_call(
        _mixer_kernel,
        grid=(BATCH, nsb),
        in_specs=[pl.BlockSpec((L_MIX, D), row),
                  pl.BlockSpec((None, 1, D), lambda b, j: (b, 0, 0)),
                  seg_in, seg_in, seg_in, kt_in, seg_in, seg_in, seg_in, seg_in,
                  pl.BlockSpec((L_MIX, LANES), row),
                  _resident((GMLP_GROUPS, GMLP_BLOCK, GMLP_BLOCK)),
                  _resident((GMLP_GROUPS, GMLP_BLOCK, LANES)),
                  _resident((1, D)),
                  _resident((D, D))],
        out_specs=pl.BlockSpec((L_MIX, D), row),
        out_shape=jax.ShapeDtypeStruct((TOKENS, D), F32),
        scratch_shapes=[pltpu.VMEM((HEADS, HEAD_DIM, HEAD_DIM), F32),
                        pltpu.VMEM((HEADS, HEAD_DIM, LANES), F32),
                        pltpu.VMEM((HEADS, SUBLANES, LANES), F32),
                        pltpu.VMEM((L_MIX, D), BF16)],
        compiler_params=pltpu.CompilerParams(dimension_semantics=("arbitrary", "arbitrary"),
                                             vmem_limit_bytes=VMEM_LIMIT),
        name="mixer",
    )(x2, g1, u, v, q, kt, mv, o, ga, gb, gates, ws, bs_rep, hn_g, w_out)


def _ffn_kernel(h_ref, sh_ref, sc_ref, g2_ref, ng_ref, w1_ref, w2_ref, fg_ref, o_ref):
    h = h_ref[...]
    xn = _rms_mod(h, ng_ref[...], sh_ref[...], sc_ref[...]).astype(BF16)
    acc = jnp.zeros((TM_FFN, D), F32)
    for c in range(FFN // FFN_CHUNK):
        cs = slice(c * FFN_CHUNK, (c + 1) * FFN_CHUNK)
        a = jnp.maximum(jnp.dot(xn, w1_ref[:, cs], preferred_element_type=F32), 0.0)
        acc = acc + jnp.dot((a * a).astype(BF16), w2_ref[cs, :], preferred_element_type=F32)
    h2 = h + g2_ref[...] * acc
    y = h2 * lax.rsqrt(jnp.mean(h2 * h2, axis=-1, keepdims=True) + EPS)
    o_ref[...] = y * fg_ref[...]


def _ffn(h1, sh2, sc2, g2, norm_g, w1, w2, final_g):
    nsb = SEQ // TM_FFN
    row = lambda i: (i, 0)
    per_batch = lambda i: (i // nsb, 0, 0)
    return pl.pallas_call(
        _ffn_kernel,
        grid=(TOKENS // TM_FFN,),
        in_specs=[pl.BlockSpec((TM_FFN, D), row),
                  pl.BlockSpec((None, 1, D), per_batch),
                  pl.BlockSpec((None, 1, D), per_batch),
                  pl.BlockSpec((None, 1, D), per_batch),
                  _resident((1, D)),
                  _resident((D, FFN)),
                  _resident((FFN, D)),
                  _resident((1, D))],
        out_specs=pl.BlockSpec((TM_FFN, D), row),
        out_shape=jax.ShapeDtypeStruct((TOKENS, D), F32),
        compiler_params=pltpu.CompilerParams(dimension_semantics=("arbitrary",),
                                             vmem_limit_bytes=VMEM_LIMIT),
        name="ffn",
    )(h1, sh2, sc2, g2, norm_g, w1, w2, final_g)


def kernel(x, c, w_ada, b_ada, norm1_g, w_in, conv_w, conv_b, mlstm_gate_b, gmlp_ln_g, gmlp_ln_b,
           gmlp_ws, gmlp_bs, mlstm_hn_g, w_out, norm2_g, w_ff1, w_ff2, final_g):
    l = 0
    mod = _ada(c, w_ada[l], b_ada[l]).reshape(BATCH, 6, 1, D)
    sh1, sc1, g1, sh2, sc2, g2 = (mod[:, t] for t in range(6))

    w = w_in[l]
    w_all = _wprep(w)
    w_gate = jnp.pad(w[:, OFF_I:OFF_GA], ((0, 0), (0, LANES - 2 * HEADS))).astype(BF16)
    gate_bias = jnp.pad(mlstm_gate_b[l].reshape(1, 2 * HEADS), ((0, 0), (0, LANES - 2 * HEADS)))
    cw = conv_w[l].reshape(CONV_K, 2, NCH, CW).transpose(1, 2, 0, 3)
    cb = conv_b[l].reshape(2, NCH, 1, CW)

    x2 = x.reshape(TOKENS, D)
    u, v, q, kt, mv, o, ga, gb, gates = _inproj(
        x2, sh1, sc1, norm1_g[l].reshape(1, D), w_all, w_gate, gate_bias, cw, cb,
        gmlp_ln_g[l].reshape(NCH, 1, CW), gmlp_ln_b[l].reshape(NCH, 1, CW))

    bs_rep = jnp.broadcast_to(gmlp_bs[l][:, :, None], (GMLP_GROUPS, GMLP_BLOCK, LANES))
    h1 = _mixer(x2, g1, u, v, q, kt, mv, o, ga, gb, gates,
                gmlp_ws[l], bs_rep, mlstm_hn_g[l].reshape(1, D), w_out[l].astype(BF16))

    out = _ffn(h1, sh2, sc2, g2, norm2_g[l].reshape(1, D),
               w_ff1[l].astype(BF16), w_ff2[l].astype(BF16), final_g.reshape(1, D))
    return out.reshape(BATCH, SEQ, D)
```

```python
import jax
import jax.numpy as jnp
from jax import lax
from jax.experimental import pallas as pl
from jax.experimental.pallas import tpu as pltpu

D = 1024
BATCH = 16
SEQ = 2048
TOKENS = BATCH * SEQ
GMLP_BLOCK = 128
GMLP_GROUPS = 8
GROUP_DIM = D // GMLP_GROUPS
HEADS = 4
HEAD_DIM = 256
CONV_K = 4
FFN = 4 * D
EPS = 1e-6
LANES = 128
SUBLANES = 8

OFF_I = 6 * D
OFF_GA = OFF_I + 2 * HEADS
N_SEG = 8
SEG_U, SEG_V, SEG_Q, SEG_K, SEG_MV, SEG_O, SEG_GA, SEG_GB = range(N_SEG)
CW = 256
NCH = D // CW

TM_IN = 512
ROWS_PIECE = 512
L_MIX = 256
TM_FFN = 1024
FFN_CHUNK = 1024
ADA_COLS = 1536

F32 = jnp.float32
BF16 = jnp.bfloat16
VMEM_LIMIT = 56 * 1024 * 1024


def _resident(shape):
    nd = len(shape)
    return pl.BlockSpec(shape, lambda *_: (0,) * nd, pipeline_mode=pl.Buffered(1))


def _ada_kernel(c_ref, w_ref, b_ref, o_ref):
    c = c_ref[...]
    ca = c * jax.nn.sigmoid(c)
    o_ref[...] = jnp.dot(ca, w_ref[...], preferred_element_type=F32,
                         precision=lax.Precision.HIGHEST) + b_ref[...]


def _ada(c, w, b):
    n = w.shape[1]
    return pl.pallas_call(
        _ada_kernel,
        grid=(n // ADA_COLS,),
        in_specs=[pl.BlockSpec((BATCH, D), lambda j: (0, 0)),
                  pl.BlockSpec((D, ADA_COLS), lambda j: (0, j)),
                  pl.BlockSpec((1, ADA_COLS), lambda j: (0, j))],
        out_specs=pl.BlockSpec((BATCH, ADA_COLS), lambda j: (0, j)),
        out_shape=jax.ShapeDtypeStruct((BATCH, n), F32),
        compiler_params=pltpu.CompilerParams(vmem_limit_bytes=VMEM_LIMIT),
        name="ada",
    )(c, w, b.reshape(1, n))


def _wprep_kernel(a_ref, b_ref, o_ref):
    s = pl.program_id(0)

    @pl.when(s < SEG_GA)
    def _():
        o_ref[...] = a_ref[...].astype(BF16)

    @pl.when(s >= SEG_GA)
    def _():
        keep = CW - 2 * HEADS
        lane = lax.broadcasted_iota(jnp.int32, (D, CW), 1)
        o_ref[...] = jnp.where(lane < keep, pltpu.roll(a_ref[...], keep, axis=1),
                               pltpu.roll(b_ref[...], keep, axis=1)).astype(BF16)


def _wprep(w):
    last = (w.shape[1] - 1) // CW
    return pl.pallas_call(
        _wprep_kernel,
        grid=(N_SEG, NCH),
        in_specs=[pl.BlockSpec((D, CW), lambda s, c: (0, s * NCH + c)),
                  pl.BlockSpec((D, CW), lambda s, c: (0, jnp.minimum(s * NCH + c + 1, last)))],
        out_specs=pl.BlockSpec((None, None, D, CW), lambda s, c: (s, c, 0, 0)),
        out_shape=jax.ShapeDtypeStruct((N_SEG, NCH, D, CW), BF16),
        compiler_params=pltpu.CompilerParams(vmem_limit_bytes=VMEM_LIMIT),
        name="wprep",
    )(w, w)


def _rms_mod(x, g, shift, scale):
    y = x * lax.rsqrt(jnp.mean(x * x, axis=-1, keepdims=True) + EPS)
    return (y * g) * (1.0 + scale) + shift


def _conv_silu(z, halo_ref, c, w, b):
    tm = z.shape[0]
    zz = jnp.concatenate([halo_ref[c], z], axis=0)
    acc = b + w[CONV_K - 1:CONV_K, :] * z
    for j in range(CONV_K - 1):
        shifted = pltpu.roll(zz, CONV_K - 1 - j, axis=0)[SUBLANES:, :]
        acc = acc + w[j:j + 1, :] * shifted
    halo_ref[c] = z[tm - SUBLANES:, :]
    return acc * jax.nn.sigmoid(acc)


def _inproj_kernel(x_ref, sh_ref, sc_ref, g_ref, w_ref, wg_ref, gb_ref, cw_ref, cb_ref,
                   lng_ref, lnb_ref,
                   u_ref, v_ref, q_ref, kt_ref, mv_ref, o_ref, ga_ref, gbo_ref, gate_ref,
                   xn_scr, gv_scr, s1_scr, halo_scr):
    i = pl.program_id(0)

    @pl.when((i % (SEQ // TM_IN)) == 0)
    def _():
        halo_scr[...] = jnp.zeros(halo_scr.shape, F32)

    xn_scr[...] = _rms_mod(x_ref[...], g_ref[...], sh_ref[...], sc_ref[...]).astype(BF16)
    s1_scr[...] = jnp.zeros(s1_scr.shape, F32)

    def seg(k, c, rs):
        return jnp.dot(xn_scr[rs, :], w_ref[k, c], preferred_element_type=F32)

    pieces = [slice(p * ROWS_PIECE, (p + 1) * ROWS_PIECE) for p in range(TM_IN // ROWS_PIECE)]

    def loop_main(c, carry):
        for rs in pieces:
            k = _conv_silu(seg(SEG_K, c, rs), halo_scr.at[1], c, cw_ref[1, c], cb_ref[1, c])
            kt_ref[c, :, rs] = k.T.astype(BF16)
            gv = jax.nn.gelu(seg(SEG_V, c, rs))
            gv_scr[c, rs, :] = gv
            s1_scr[rs, :] += gv[:, :LANES] + gv[:, LANES:]
            q = _conv_silu(seg(SEG_Q, c, rs), halo_scr.at[0], c, cw_ref[0, c], cb_ref[0, c])
            q_ref[c, rs, :] = (q * (HEAD_DIM ** -0.5)).astype(BF16)
            u_ref[c, rs, :] = jax.nn.gelu(seg(SEG_U, c, rs)).astype(BF16)
            o_ref[c, rs, :] = jax.nn.sigmoid(seg(SEG_O, c, rs)).astype(BF16)
            ga_ref[c, rs, :] = jax.nn.sigmoid(seg(SEG_GA, c, rs)).astype(BF16)
            gbo_ref[c, rs, :] = jax.nn.sigmoid(seg(SEG_GB, c, rs)).astype(BF16)
            mv_ref[c, rs, :] = seg(SEG_MV, c, rs).astype(BF16)
        return carry

    lax.fori_loop(0, NCH, loop_main, 0)

    gp = jnp.dot(xn_scr[...], wg_ref[...], preferred_element_type=F32) + gb_ref[...]
    lane = lax.broadcasted_iota(jnp.int32, gp.shape, 1)
    gate_ref[...] = jnp.where(lane < HEADS, gp, jax.nn.log_sigmoid(gp))

    mu = jnp.sum(s1_scr[...], axis=-1, keepdims=True) * (1.0 / D)
    ssq = jnp.zeros((TM_IN, LANES), F32)
    for c in range(NCH):
        dv = gv_scr[c] - mu
        dv = dv * dv
        ssq = ssq + (dv[:, :LANES] + dv[:, LANES:])
    rstd = lax.rsqrt(jnp.sum(ssq, axis=-1, keepdims=True) * (1.0 / D) + EPS)
    for c in range(NCH):
        v_ref[c] = (((gv_scr[c] - mu) * rstd) * lng_ref[c] + lnb_ref[c]).astype(BF16)


def _inproj(x2, sh1, sc1, norm_g, w_all, w_gate, gate_bias, conv_w, conv_b, ln_g, ln_b):
    nsb = SEQ // TM_IN
    row = lambda i: (i, 0)
    per_batch = lambda i: (i // nsb, 0, 0)
    seg_out = pl.BlockSpec((NCH, TM_IN, CW), lambda i: (0, i, 0))
    seg_shape = jax.ShapeDtypeStruct((NCH, TOKENS, CW), BF16)
    seg_outs = [seg_out] * N_SEG
    seg_shapes = [seg_shape] * N_SEG
    seg_outs[SEG_K] = pl.BlockSpec((NCH, CW, TM_IN), lambda i: (0, 0, i))
    seg_shapes[SEG_K] = jax.ShapeDtypeStruct((NCH, CW, TOKENS), BF16)
    return pl.pallas_call(
        _inproj_kernel,
        grid=(TOKENS // TM_IN,),
        in_specs=[pl.BlockSpec((TM_IN, D), row),
                  pl.BlockSpec((None, 1, D), per_batch),
                  pl.BlockSpec((None, 1, D), per_batch),
                  _resident((1, D)),
                  _resident((N_SEG, NCH, D, CW)),
                  _resident((D, LANES)),
                  _resident((1, LANES)),
                  _resident((2, NCH, CONV_K, CW)),
                  _resident((2, NCH, 1, CW)),
                  _resident((NCH, 1, CW)),
                  _resident((NCH, 1, CW))],
        out_specs=seg_outs + [pl.BlockSpec((TM_IN, LANES), row)],
        out_shape=seg_shapes + [jax.ShapeDtypeStruct((TOKENS, LANES), F32)],
        scratch_shapes=[pltpu.VMEM((TM_IN, D), BF16),
                        pltpu.VMEM((NCH, TM_IN, CW), F32),
                        pltpu.VMEM((TM_IN, LANES), F32),
                        pltpu.VMEM((2, NCH, SUBLANES, CW), F32)],
        compiler_params=pltpu.CompilerParams(dimension_semantics=("arbitrary",),
                                             vmem_limit_bytes=VMEM_LIMIT),
        name="inproj",
    )(x2, sh1, sc1, norm_g, w_all, w_gate, gate_bias, conv_w, conv_b, ln_g, ln_b)


def _cumsum_rows(x):
    n = x.shape[0]
    row = lax.broadcasted_iota(jnp.int32, x.shape, 0)
    k = 1
    while k < n:
        x = x + jnp.where(row >= k, pltpu.roll(x, k, axis=0), 0.0)
        k *= 2
    return x


def _rep2(a):
    return jnp.concatenate([a, a], axis=1)


def _mixer_kernel(x_ref, g1_ref, u_ref, v_ref, q_ref, kt_ref, mv_ref, o_ref, ga_ref, gb_ref,
                  gate_ref, ws_ref, bsb_ref, hng_ref, wout_ref,
                  h1_ref,
                  c_scr, n_scr, m_scr, merged_scr):
    L = L_MIX
    j = pl.program_id(1)

    @pl.when(j == 0)
    def _():
        c_scr[...] = jnp.zeros(c_scr.shape, F32)
        n_scr[...] = jnp.zeros(n_scr.shape, F32)
        m_scr[...] = jnp.zeros(m_scr.shape, F32)

    gates = gate_ref[...]
    bcum = _cumsum_rows(gates)
    gates_t = gates.T
    bcum_t = bcum.T
    row = lax.broadcasted_iota(jnp.int32, (L, L), 0)
    col = lax.broadcasted_iota(jnp.int32, (L, L), 1)
    causal = col <= row
    ones_rhs = jnp.ones((L, LANES), BF16)

    brow = lax.broadcasted_iota(jnp.int32, (GMLP_BLOCK, GMLP_BLOCK), 0) // 64
    bcol = lax.broadcasted_iota(jnp.int32, (GMLP_BLOCK, GMLP_BLOCK), 1) // 64
    wmask = bcol <= brow
    nblk = L // GMLP_BLOCK

    for h in range(HEADS):
        hs = slice(h * HEAD_DIM, (h + 1) * HEAD_DIM)
        li_r = gates_t[h:h + 1, :]
        b_r = bcum_t[HEADS + h:HEADS + h + 1, :]
        b_c = jnp.broadcast_to(bcum[:, HEADS + h:HEADS + h + 1], (L, LANES))
        m_prev = m_scr[h, 0:1, 0:1]
        dmat = jnp.where(causal, _rep2(b_c) + (li_r - b_r), -jnp.inf)
        m_inter = b_c + m_prev
        m_t = jnp.maximum(m_inter, jnp.max(dmat, axis=-1, keepdims=True))
        wts = jnp.exp(dmat - _rep2(m_t))
        qh = q_ref[h]
        kt = kt_ref[h]
        vh = mv_ref[h]
        s = jnp.dot(qh, kt, preferred_element_type=F32) * wts
        sb = s.astype(BF16)
        inter = jnp.exp(m_inter - m_t)
        c_old = c_scr[h]
        n_old = n_scr[h]
        n_hi = n_old.astype(BF16)
        n_lo = (n_old - n_hi.astype(F32)).astype(BF16)
        qn2 = jnp.dot(qh, jnp.concatenate([n_hi, n_lo], axis=1), preferred_element_type=F32)
        qn = qn2[:, :LANES] + qn2[:, LANES:]
        num = (jnp.dot(sb, vh, preferred_element_type=F32)
               + _rep2(inter) * jnp.dot(qh, c_old.astype(BF16), preferred_element_type=F32))
        den = jnp.dot(sb, ones_rhs, preferred_element_type=F32) + inter * qn
        inv = 1.0 / jnp.maximum(jnp.abs(den), jnp.exp(-m_t))
        hh = num * _rep2(inv)

        b_last = b_r[:, L - 1:L]
        g_r = (b_last - b_r) + li_r
        m_new = jnp.maximum(b_last + m_prev, jnp.max(g_r, axis=-1, keepdims=True))
        decay = jnp.exp((b_last + m_prev) - m_new)
        wkt = jnp.exp(g_r - m_new).astype(BF16) * kt
        c_scr[h] = decay * c_old + jnp.dot(wkt, vh, preferred_element_type=F32)
        n_scr[h] = decay * n_old + jnp.dot(wkt, ones_rhs, preferred_element_type=F32)
        m_scr[h] = jnp.broadcast_to(m_new, (SUBLANES, LANES))

        mu = jnp.mean(hh, axis=-1, keepdims=True)
        dh = hh - mu
        var = jnp.mean(dh * dh, axis=-1, keepdims=True)
        y_n = (dh * lax.rsqrt(var + EPS)) * hng_ref[:, hs]

        for half in range(CW // GROUP_DIM):
            g = h * (CW // GROUP_DIM) + half
            ls = slice(half * GROUP_DIM, (half + 1) * GROUP_DIM)
            wsg = jnp.where(wmask, ws_ref[g], 0.0).astype(BF16)
            vcat = jnp.concatenate(
                [v_ref[h, n * GMLP_BLOCK:(n + 1) * GMLP_BLOCK, ls] for n in range(nblk)], axis=1)
            mixed = jnp.dot(wsg, vcat, preferred_element_type=F32) + _rep2(bsb_ref[g])
            for n in range(nblk):
                rs = slice(n * GMLP_BLOCK, (n + 1) * GMLP_BLOCK)
                gate_a = (ga_ref[h, rs, ls] * u_ref[h, rs, ls]).astype(F32)
                gate_b = (gb_ref[h, rs, ls] * o_ref[h, rs, ls]).astype(F32)
                merged = (gate_a * mixed[:, n * GMLP_BLOCK:(n + 1) * GMLP_BLOCK]
                          + gate_b * y_n[rs, ls])
                merged_scr[rs, g * GROUP_DIM:(g + 1) * GROUP_DIM] = merged.astype(BF16)

    proj = jnp.dot(merged_scr[...], wout_ref[...], preferred_element_type=F32)
    h1_ref[...] = x_ref[...] + g1_ref[...] * proj


def _mixer(x2, g1, u, v, q, kt, mv, o, ga, gb, gates, ws, bs_rep, hn_g, w_out):
    nsb = SEQ // L_MIX
    row = lambda b, j: (b * nsb + j, 0)
    seg_in = pl.BlockSpec((NCH, L_MIX, CW), lambda b, j: (0, b * nsb + j, 0))
    kt_in = pl.BlockSpec((NCH, CW, L_MIX), lambda b, j: (0, 0, b * nsb + j))
    return pl.pallas_call(
        _mixer_kernel,
        grid=(BATCH, nsb),
        in_specs=[pl.BlockSpec((L_MIX, D), row),
                  pl.BlockSpec((None, 1, D), lambda b, j: (b, 0, 0)),
                  seg_in, seg_in, seg_in, kt_in, seg_in, seg_in, seg_in, seg_in,
                  pl.BlockSpec((L_MIX, LANES), row),
                  _resident((GMLP_GROUPS, GMLP_BLOCK, GMLP_BLOCK)),
                  _resident((GMLP_GROUPS, GMLP_BLOCK, LANES)),
                  _resident((1, D)),
                  _resident((D, D))],
        out_specs=pl.BlockSpec((L_MIX, D), row),
        out_shape=jax.ShapeDtypeStruct((TOKENS, D), F32),
        scratch_shapes=[pltpu.VMEM((HEADS, HEAD_DIM, HEAD_DIM), F32),
                        pltpu.VMEM((HEADS, HEAD_DIM, LANES), F32),
                        pltpu.VMEM((HEADS, SUBLANES, LANES), F32),
                        pltpu.VMEM((L_MIX, D), BF16)],
        compiler_params=pltpu.CompilerParams(dimension_semantics=("arbitrary", "arbitrary"),
                                             vmem_limit_bytes=VMEM_LIMIT),
        name="mixer",
    )(x2, g1, u, v, q, kt, mv, o, ga, gb, gates, ws, bs_rep, hn_g, w_out)


def _ffn_kernel(h_ref, sh_ref, sc_ref, g2_ref, ng_ref, w1_ref, w2_ref, fg_ref, o_ref):
    h = h_ref[...]
    xn = _rms_mod(h, ng_ref[...], sh_ref[...], sc_ref[...]).astype(BF16)
    acc = jnp.zeros((TM_FFN, D), F32)
    for c in range(FFN // FFN_CHUNK):
        cs = slice(c * FFN_CHUNK, (c + 1) * FFN_CHUNK)
        a = jnp.maximum(jnp.dot(xn, w1_ref[:, cs], preferred_element_type=F32), 0.0)
        acc = acc + jnp.dot((a * a).astype(BF16), w2_ref[cs, :], preferred_element_type=F32)
    h2 = h + g2_ref[...] * acc
    y = h2 * lax.rsqrt(jnp.mean(h2 * h2, axis=-1, keepdims=True) + EPS)
    o_ref[...] = y * fg_ref[...]


def _ffn(h1, sh2, sc2, g2, norm_g, w1, w2, final_g):
    nsb = SEQ // TM_FFN
    row = lambda i: (i, 0)
    per_batch = lambda i: (i // nsb, 0, 0)
    return pl.pallas_call(
        _ffn_kernel,
        grid=(TOKENS // TM_FFN,),
        in_specs=[pl.BlockSpec((TM_FFN, D), row),
                  pl.BlockSpec((None, 1, D), per_batch),
                  pl.BlockSpec((None, 1, D), per_batch),
                  pl.BlockSpec((None, 1, D), per_batch),
                  _resident((1, D)),
                  _resident((D, FFN)),
                  _resident((FFN, D)),
                  _resident((1, D))],
        out_specs=pl.BlockSpec((TM_FFN, D), row),
        out_shape=jax.ShapeDtypeStruct((TOKENS, D), F32),
        compiler_params=pltpu.CompilerParams(dimension_semantics=("arbitrary",),
                                             vmem_limit_bytes=VMEM_LIMIT),
        name="ffn",
    )(h1, sh2, sc2, g2, norm_g, w1, w2, final_g)


def kernel(x, c, w_ada, b_ada, norm1_g, w_in, conv_w, conv_b, mlstm_gate_b, gmlp_ln_g, gmlp_ln_b,
           gmlp_ws, gmlp_bs, mlstm_hn_g, w_out, norm2_g, w_ff1, w_ff2, final_g):
    l = 0
    mod = _ada(c, w_ada[l], b_ada[l]).reshape(BATCH, 6, 1, D)
    sh1, sc1, g1, sh2, sc2, g2 = (mod[:, t] for t in range(6))

    w = w_in[l]
    w_all = _wprep(w)
    w_gate = jnp.pad(w[:, OFF_I:OFF_GA], ((0, 0), (0, LANES - 2 * HEADS))).astype(BF16)
    gate_bias = jnp.pad(mlstm_gate_b[l].reshape(1, 2 * HEADS), ((0, 0), (0, LANES - 2 * HEADS)))
    cw = conv_w[l].reshape(CONV_K, 2, NCH, CW).transpose(1, 2, 0, 3)
    cb = conv_b[l].reshape(2, NCH, 1, CW)

    x2 = x.reshape(TOKENS, D)
    u, v, q, kt, mv, o, ga, gb, gates = _inproj(
        x2, sh1, sc1, norm1_g[l].reshape(1, D), w_all, w_gate, gate_bias, cw, cb,
        gmlp_ln_g[l].reshape(NCH, 1, CW), gmlp_ln_b[l].reshape(NCH, 1, CW))

    bs_rep = jnp.broadcast_to(gmlp_bs[l][:, :, None], (GMLP_GROUPS, GMLP_BLOCK, LANES))
    h1 = _mixer(x2, g1, u, v, q, kt, mv, o, ga, gb, gates,
                gmlp_ws[l], bs_rep, mlstm_hn_g[l].reshape(1, D), w_out[l].astype(BF16))

    out = _ffn(h1, sh2, sc2, g2, norm2_g[l].reshape(1, D),
               w_ff1[l].astype(BF16), w_ff2[l].astype(BF16), final_g.reshape(1, D))
    return out.reshape(BATCH, SEQ, D)
```

```python
import jax
import jax.numpy as jnp
from jax import lax
from jax.experimental import pallas as pl
from jax.experimental.pallas import tpu as pltpu

D = 1024
BATCH = 16
SEQ = 2048
TOKENS = BATCH * SEQ
GMLP_BLOCK = 128
GMLP_GROUPS = 8
GROUP_DIM = D // GMLP_GROUPS
HEADS = 4
HEAD_DIM = 256
CONV_K = 4
FFN = 4 * D
EPS = 1e-6
LANES = 128
SUBLANES = 8

OFF_I = 6 * D
OFF_GA = OFF_I + 2 * HEADS
N_SEG = 8
SEG_U, SEG_V, SEG_Q, SEG_K, SEG_MV, SEG_O, SEG_GA, SEG_GB = range(N_SEG)
CW = 256
NCH = D // CW

TM_IN = 512
ROWS_PIECE = 512
L_MIX = 256
TM_FFN = 512
FFN_CHUNK = 1024
ADA_COLS = 1536

F32 = jnp.float32
BF16 = jnp.bfloat16
VMEM_LIMIT = 56 * 1024 * 1024


def _resident(shape):
    nd = len(shape)
    return pl.BlockSpec(shape, lambda *_: (0,) * nd, pipeline_mode=pl.Buffered(1))


def _ada_kernel(c_ref, w_ref, b_ref, o_ref):
    c = c_ref[...]
    ca = c * jax.nn.sigmoid(c)
    o_ref[...] = jnp.dot(ca, w_ref[...], preferred_element_type=F32,
                         precision=lax.Precision.HIGHEST) + b_ref[...]


def _ada(c, w, b):
    n = w.shape[1]
    return pl.pallas_call(
        _ada_kernel,
        grid=(n // ADA_COLS,),
        in_specs=[pl.BlockSpec((BATCH, D), lambda j: (0, 0)),
                  pl.BlockSpec((D, ADA_COLS), lambda j: (0, j)),
                  pl.BlockSpec((1, ADA_COLS), lambda j: (0, j))],
        out_specs=pl.BlockSpec((BATCH, ADA_COLS), lambda j: (0, j)),
        out_shape=jax.ShapeDtypeStruct((BATCH, n), F32),
        compiler_params=pltpu.CompilerParams(vmem_limit_bytes=VMEM_LIMIT),
        name="ada",
    )(c, w, b.reshape(1, n))


def _wprep_kernel(a_ref, b_ref, o_ref):
    s = pl.program_id(0)

    @pl.when(s < SEG_GA)
    def _():
        o_ref[...] = a_ref[...].T.astype(BF16)

    @pl.when(s >= SEG_GA)
    def _():
        rows = jnp.concatenate([a_ref[2 * HEADS:, :], b_ref[...]], axis=0)
        o_ref[...] = rows.T.astype(BF16)


def _wprep(wt):
    gate_rows = 2 * HEADS
    return pl.pallas_call(
        _wprep_kernel,
        grid=(N_SEG, NCH),
        in_specs=[pl.BlockSpec((CW, D), lambda s, c: (s * NCH + c, 0)),
                  pl.BlockSpec((gate_rows, D),
                               lambda s, c: ((s * NCH + c + 1) * (CW // gate_rows), 0))],
        out_specs=pl.BlockSpec((None, None, D, CW), lambda s, c: (s, c, 0, 0)),
        out_shape=jax.ShapeDtypeStruct((N_SEG, NCH, D, CW), BF16),
        compiler_params=pltpu.CompilerParams(vmem_limit_bytes=VMEM_LIMIT),
        name="wprep",
    )(wt, wt)


def _rms_mod(x, g, shift, scale):
    y = x * lax.rsqrt(jnp.mean(x * x, axis=-1, keepdims=True) + EPS)
    return (y * g) * (1.0 + scale) + shift


def _conv_silu(z, halo_ref, c, w, b):
    tm = z.shape[0]
    zz = jnp.concatenate([halo_ref[c], z], axis=0)
    acc = b + w[CONV_K - 1:CONV_K, :] * z
    for j in range(CONV_K - 1):
        shifted = pltpu.roll(zz, CONV_K - 1 - j, axis=0)[SUBLANES:, :]
        acc = acc + w[j:j + 1, :] * shifted
    halo_ref[c] = z[tm - SUBLANES:, :]
    return acc * jax.nn.sigmoid(acc)


def _inproj_kernel(x_ref, sh_ref, sc_ref, g_ref, w_ref, wg_ref, gb_ref, cw_ref, cb_ref,
                   lng_ref, lnb_ref,
                   u_ref, v_ref, q_ref, kt_ref, mv_ref, o_ref, ga_ref, gbo_ref, gate_ref,
                   xn_scr, gv_scr, s1_scr, halo_scr):
    i = pl.program_id(0)

    @pl.when((i % (SEQ // TM_IN)) == 0)
    def _():
        halo_scr[...] = jnp.zeros(halo_scr.shape, F32)

    xn_scr[...] = _rms_mod(x_ref[...], g_ref[...], sh_ref[...], sc_ref[...]).astype(BF16)
    s1_scr[...] = jnp.zeros(s1_scr.shape, F32)

    def seg(k, c, rs):
        return jnp.dot(xn_scr[rs, :], w_ref[k, c], preferred_element_type=F32)

    pieces = [slice(p * ROWS_PIECE, (p + 1) * ROWS_PIECE) for p in range(TM_IN // ROWS_PIECE)]

    def loop_main(c, carry):
        for rs in pieces:
            k = _conv_silu(seg(SEG_K, c, rs), halo_scr.at[1], c, cw_ref[1, c], cb_ref[1, c])
            kt_ref[c, :, rs] = k.T.astype(BF16)
            gv = jax.nn.gelu(seg(SEG_V, c, rs))
            gv_scr[c, rs, :] = gv
            s1_scr[rs, :] += gv[:, :LANES] + gv[:, LANES:]
            q = _conv_silu(seg(SEG_Q, c, rs), halo_scr.at[0], c, cw_ref[0, c], cb_ref[0, c])
            q_ref[c, rs, :] = (q * (HEAD_DIM ** -0.5)).astype(BF16)
            u_ref[c, rs, :] = jax.nn.gelu(seg(SEG_U, c, rs)).astype(BF16)
            o_ref[c, rs, :] = jax.nn.sigmoid(seg(SEG_O, c, rs)).astype(BF16)
            ga_ref[c, rs, :] = jax.nn.sigmoid(seg(SEG_GA, c, rs)).astype(BF16)
            gbo_ref[c, rs, :] = jax.nn.sigmoid(seg(SEG_GB, c, rs)).astype(BF16)
            mv_ref[c, rs, :] = seg(SEG_MV, c, rs).astype(BF16)
        return carry

    lax.fori_loop(0, NCH, loop_main, 0)

    wg = jnp.concatenate([wg_ref[...], jnp.zeros((LANES - 2 * HEADS, D), F32)], axis=0)
    gp = lax.dot_general(xn_scr[...], wg.astype(BF16), (((1,), (1,)), ((), ())),
                         preferred_element_type=F32) + gb_ref[...]
    lane = lax.broadcasted_iota(jnp.int32, gp.shape, 1)
    gate_ref[...] = jnp.where(lane < HEADS, gp, jax.nn.log_sigmoid(gp))

    mu = jnp.sum(s1_scr[...], axis=-1, keepdims=True) * (1.0 / D)
    ssq = jnp.zeros((TM_IN, LANES), F32)
    for c in range(NCH):
        dv = gv_scr[c] - mu
        dv = dv * dv
        ssq = ssq + (dv[:, :LANES] + dv[:, LANES:])
    rstd = lax.rsqrt(jnp.sum(ssq, axis=-1, keepdims=True) * (1.0 / D) + EPS)
    for c in range(NCH):
        v_ref[c] = (((gv_scr[c] - mu) * rstd) * lng_ref[c] + lnb_ref[c]).astype(BF16)


def _inproj(x2, sh1, sc1, norm_g, w_all, w_gate, gate_bias, conv_w, conv_b, ln_g, ln_b):
    nsb = SEQ // TM_IN
    row = lambda i: (i, 0)
    per_batch = lambda i: (i // nsb, 0, 0)
    seg_out = pl.BlockSpec((NCH, TM_IN, CW), lambda i: (0, i, 0))
    seg_shape = jax.ShapeDtypeStruct((NCH, TOKENS, CW), BF16)
    seg_outs = [seg_out] * N_SEG
    seg_shapes = [seg_shape] * N_SEG
    seg_outs[SEG_K] = pl.BlockSpec((NCH, CW, TM_IN), lambda i: (0, 0, i))
    seg_shapes[SEG_K] = jax.ShapeDtypeStruct((NCH, CW, TOKENS), BF16)
    return pl.pallas_call(
        _inproj_kernel,
        grid=(TOKENS // TM_IN,),
        in_specs=[pl.BlockSpec((TM_IN, D), row),
                  pl.BlockSpec((None, 1, D), per_batch),
                  pl.BlockSpec((None, 1, D), per_batch),
                  _resident((1, D)),
                  _resident((N_SEG, NCH, D, CW)),
                  _resident((2 * HEADS, D)),
                  _resident((1, LANES)),
                  _resident((2, NCH, CONV_K, CW)),
                  _resident((2, NCH, 1, CW)),
                  _resident((NCH, 1, CW)),
                  _resident((NCH, 1, CW))],
        out_specs=seg_outs + [pl.BlockSpec((TM_IN, LANES), row)],
        out_shape=seg_shapes + [jax.ShapeDtypeStruct((TOKENS, LANES), F32)],
        scratch_shapes=[pltpu.VMEM((TM_IN, D), BF16),
                        pltpu.VMEM((NCH, TM_IN, CW), F32),
                        pltpu.VMEM((TM_IN, LANES), F32),
                        pltpu.VMEM((2, NCH, SUBLANES, CW), F32)],
        compiler_params=pltpu.CompilerParams(dimension_semantics=("arbitrary",),
                                             vmem_limit_bytes=VMEM_LIMIT),
        name="inproj",
    )(x2, sh1, sc1, norm_g, w_all, w_gate, gate_bias, conv_w, conv_b, ln_g, ln_b)


def _cumsum_rows(x):
    n = x.shape[0]
    row = lax.broadcasted_iota(jnp.int32, x.shape, 0)
    k = 1
    while k < n:
        x = x + jnp.where(row >= k, pltpu.roll(x, k, axis=0), 0.0)
        k *= 2
    return x


def _rep2(a):
    return jnp.concatenate([a, a], axis=1)


def _mixer_kernel(x_ref, g1_ref, u_ref, v_ref, q_ref, kt_ref, mv_ref, o_ref, ga_ref, gb_ref,
                  gate_ref, ws_ref, bsb_ref, hng_ref, wout_ref,
                  h1_ref,
                  c_scr, n_scr, m_scr, merged_scr):
    L = L_MIX
    j = pl.program_id(1)

    @pl.when(j == 0)
    def _():
        c_scr[...] = jnp.zeros(c_scr.shape, F32)
        n_scr[...] = jnp.zeros(n_scr.shape, F32)
        m_scr[...] = jnp.zeros(m_scr.shape, F32)

    gates = gate_ref[...]
    bcum = _cumsum_rows(gates)
    gates_t = gates.T
    bcum_t = bcum.T
    row = lax.broadcasted_iota(jnp.int32, (L, L), 0)
    col = lax.broadcasted_iota(jnp.int32, (L, L), 1)
    causal = col <= row
    ones_rhs = jnp.ones((L, LANES), BF16)

    brow = lax.broadcasted_iota(jnp.int32, (GMLP_BLOCK, GMLP_BLOCK), 0) // 64
    bcol = lax.broadcasted_iota(jnp.int32, (GMLP_BLOCK, GMLP_BLOCK), 1) // 64
    wmask = bcol <= brow
    nblk = L // GMLP_BLOCK

    for h in range(HEADS):
        hs = slice(h * HEAD_DIM, (h + 1) * HEAD_DIM)
        li_r = gates_t[h:h + 1, :]
        b_r = bcum_t[HEADS + h:HEADS + h + 1, :]
        b_c = jnp.broadcast_to(bcum[:, HEADS + h:HEADS + h + 1], (L, LANES))
        m_prev = m_scr[h, 0:1, 0:1]
        dmat = jnp.where(causal, _rep2(b_c) + (li_r - b_r), -jnp.inf)
        m_inter = b_c + m_prev
        m_t = jnp.maximum(m_inter, jnp.max(dmat, axis=-1, keepdims=True))
        wts = jnp.exp(dmat - _rep2(m_t))
        qh = q_ref[h]
        kt = kt_ref[h]
        vh = mv_ref[h]
        s = jnp.dot(qh, kt, preferred_element_type=F32) * wts
        sb = s.astype(BF16)
        inter = jnp.exp(m_inter - m_t)
        c_old = c_scr[h]
        n_old = n_scr[h]
        n_hi = n_old.astype(BF16)
        n_lo = (n_old - n_hi.astype(F32)).astype(BF16)
        qn2 = jnp.dot(qh, jnp.concatenate([n_hi, n_lo], axis=1), preferred_element_type=F32)
        qn = qn2[:, :LANES] + qn2[:, LANES:]
        num = (jnp.dot(sb, vh, preferred_element_type=F32)
               + _rep2(inter) * jnp.dot(qh, c_old.astype(BF16), preferred_element_type=F32))
        den = jnp.dot(sb, ones_rhs, preferred_element_type=F32) + inter * qn
        inv = 1.0 / jnp.maximum(jnp.abs(den), jnp.exp(-m_t))
        hh = num * _rep2(inv)

        b_last = b_r[:, L - 1:L]
        g_r = (b_last - b_r) + li_r
        m_new = jnp.maximum(b_last + m_prev, jnp.max(g_r, axis=-1, keepdims=True))
        decay = jnp.exp((b_last + m_prev) - m_new)
        wkt = jnp.exp(g_r - m_new).astype(BF16) * kt
        c_scr[h] = decay * c_old + jnp.dot(wkt, vh, preferred_element_type=F32)
        n_scr[h] = decay * n_old + jnp.dot(wkt, ones_rhs, preferred_element_type=F32)
        m_scr[h] = jnp.broadcast_to(m_new, (SUBLANES, LANES))

        mu = jnp.mean(hh, axis=-1, keepdims=True)
        dh = hh - mu
        var = jnp.mean(dh * dh, axis=-1, keepdims=True)
        y_n = (dh * lax.rsqrt(var + EPS)) * hng_ref[:, hs]

        for half in range(CW // GROUP_DIM):
            g = h * (CW // GROUP_DIM) + half
            ls = slice(half * GROUP_DIM, (half + 1) * GROUP_DIM)
            wsg = jnp.where(wmask, ws_ref[g], 0.0).astype(BF16)
            vcat = jnp.concatenate(
                [v_ref[h, n * GMLP_BLOCK:(n + 1) * GMLP_BLOCK, ls] for n in range(nblk)], axis=1)
            mixed = jnp.dot(wsg, vcat, preferred_element_type=F32) + _rep2(bsb_ref[g])
            for n in range(nblk):
                rs = slice(n * GMLP_BLOCK, (n + 1) * GMLP_BLOCK)
                gate_a = (ga_ref[h, rs, ls] * u_ref[h, rs, ls]).astype(F32)
                gate_b = (gb_ref[h, rs, ls] * o_ref[h, rs, ls]).astype(F32)
                merged = (gate_a * mixed[:, n * GMLP_BLOCK:(n + 1) * GMLP_BLOCK]
                          + gate_b * y_n[rs, ls])
                merged_scr[rs, g * GROUP_DIM:(g + 1) * GROUP_DIM] = merged.astype(BF16)

    proj = jnp.dot(merged_scr[...], wout_ref[...], preferred_element_type=F32)
    h1_ref[...] = x_ref[...] + g1_ref[...] * proj


def _mixer(x2, g1, u, v, q, kt, mv, o, ga, gb, gates, ws, bs_rep, hn_g, w_out):
    nsb = SEQ // L_MIX
    row = lambda b, j: (b * nsb + j, 0)
    seg_in = pl.BlockSpec((NCH, L_MIX, CW), lambda b, j: (0, b * nsb + j, 0))
    kt_in = pl.BlockSpec((NCH, CW, L_MIX), lambda b, j: (0, 0, b * nsb + j))
    return pl.pallas_call(
        _mixer_kernel,
        grid=(BATCH, nsb),
        in_specs=[pl.BlockSpec((L_MIX, D), row),
                  pl.BlockSpec((None, 1, D), lambda b, j: (b, 0, 0)),
                  seg_in, seg_in, seg_in, kt_in, seg_in, seg_in, seg_in, seg_in,
                  pl.BlockSpec((L_MIX, LANES), row),
                  _resident((GMLP_GROUPS, GMLP_BLOCK, GMLP_BLOCK)),
                  _resident((GMLP_GROUPS, GMLP_BLOCK, LANES)),
                  _resident((1, D)),
                  _resident((D, D))],
        out_specs=pl.BlockSpec((L_MIX, D), row),
        out_shape=jax.ShapeDtypeStruct((TOKENS, D), F32),
        scratch_shapes=[pltpu.VMEM((HEADS, HEAD_DIM, HEAD_DIM), F32),
                        pltpu.VMEM((HEADS, HEAD_DIM, LANES), F32),
                        pltpu.VMEM((HEADS, SUBLANES, LANES), F32),
                        pltpu.VMEM((L_MIX, D), BF16)],
        compiler_params=pltpu.CompilerParams(dimension_semantics=("arbitrary", "arbitrary"),
                                             vmem_limit_bytes=VMEM_LIMIT),
        name="mixer",
    )(x2, g1, u, v, q, kt, mv, o, ga, gb, gates, ws, bs_rep, hn_g, w_out)


def _ffn_kernel(h_ref, sh_ref, sc_ref, g2_ref, ng_ref, w1_ref, w2_ref, fg_ref, o_ref):
    h = h_ref[...]
    xn = _rms_mod(h, ng_ref[...], sh_ref[...], sc_ref[...]).astype(BF16)
    acc = jnp.zeros((TM_FFN, D), F32)
    for c in range(FFN // FFN_CHUNK):
        cs = slice(c * FFN_CHUNK, (c + 1) * FFN_CHUNK)
        a = jnp.maximum(jnp.dot(xn, w1_ref[:, cs], preferred_element_type=F32), 0.0)
        acc = acc + jnp.dot((a * a).astype(BF16), w2_ref[cs, :], preferred_element_type=F32)
    h2 = h + g2_ref[...] * acc
    y = h2 * lax.rsqrt(jnp.mean(h2 * h2, axis=-1, keepdims=True) + EPS)
    o_ref[...] = y * fg_ref[...]


def _ffn(h1, sh2, sc2, g2, norm_g, w1, w2, final_g):
    nsb = SEQ // TM_FFN
    row = lambda i: (i, 0)
    per_batch = lambda i: (i // nsb, 0, 0)
    return pl.pallas_call(
        _ffn_kernel,
        grid=(TOKENS // TM_FFN,),
        in_specs=[pl.BlockSpec((TM_FFN, D), row),
                  pl.BlockSpec((None, 1, D), per_batch),
                  pl.BlockSpec((None, 1, D), per_batch),
                  pl.BlockSpec((None, 1, D), per_batch),
                  _resident((1, D)),
                  _resident((D, FFN)),
                  _resident((FFN, D)),
                  _resident((1, D))],
        out_specs=pl.BlockSpec((TM_FFN, D), row),
        out_shape=jax.ShapeDtypeStruct((TOKENS, D), F32),
        compiler_params=pltpu.CompilerParams(dimension_semantics=("arbitrary",),
                                             vmem_limit_bytes=VMEM_LIMIT),
        name="ffn",
    )(h1, sh2, sc2, g2, norm_g, w1, w2, final_g)


def kernel(x, c, w_ada, b_ada, norm1_g, w_in, conv_w, conv_b, mlstm_gate_b, gmlp_ln_g, gmlp_ln_b,
           gmlp_ws, gmlp_bs, mlstm_hn_g, w_out, norm2_g, w_ff1, w_ff2, final_g):
    l = 0
    mod = _ada(c, w_ada[l], b_ada[l]).reshape(BATCH, 6, 1, D)
    sh1, sc1, g1, sh2, sc2, g2 = (mod[:, t] for t in range(6))

    wt = jnp.swapaxes(w_in, 1, 2)[l]
    w_all = _wprep(wt)
    w_gate = wt[OFF_I:OFF_GA]
    gate_bias = jnp.pad(mlstm_gate_b[l].reshape(1, 2 * HEADS), ((0, 0), (0, LANES - 2 * HEADS)))
    cw = conv_w[l].reshape(CONV_K, 2, NCH, CW).transpose(1, 2, 0, 3)
    cb = conv_b[l].reshape(2, NCH, 1, CW)

    x2 = x.reshape(TOKENS, D)
    u, v, q, kt, mv, o, ga, gb, gates = _inproj(
        x2, sh1, sc1, norm1_g[l].reshape(1, D), w_all, w_gate, gate_bias, cw, cb,
        gmlp_ln_g[l].reshape(NCH, 1, CW), gmlp_ln_b[l].reshape(NCH, 1, CW))

    bs_rep = jnp.broadcast_to(gmlp_bs[l][:, :, None], (GMLP_GROUPS, GMLP_BLOCK, LANES))
    h1 = _mixer(x2, g1, u, v, q, kt, mv, o, ga, gb, gates,
                gmlp_ws[l], bs_rep, mlstm_hn_g[l].reshape(1, D), w_out[l].astype(BF16))

    out = _ffn(h1, sh2, sc2, g2, norm2_g[l].reshape(1, D),
               w_ff1[l].astype(BF16), w_ff2[l].astype(BF16), final_g.reshape(1, D))
    return out.reshape(BATCH, SEQ, D)
```

```python
import jax
import jax.numpy as jnp
from jax import lax
from jax.experimental import pallas as pl
from jax.experimental.pallas import tpu as pltpu

D = 1024
BATCH = 16
SEQ = 2048
TOKENS = BATCH * SEQ
GMLP_BLOCK = 128
GMLP_GROUPS = 8
GROUP_DIM = D // GMLP_GROUPS
HEADS = 4
HEAD_DIM = 256
CONV_K = 4
FFN = 4 * D
EPS = 1e-6
LANES = 128
SUBLANES = 8

OFF_I = 6 * D
OFF_GA = OFF_I + 2 * HEADS
N_SEG = 8
SEG_U, SEG_V, SEG_Q, SEG_K, SEG_MV, SEG_O, SEG_GA, SEG_GB = range(N_SEG)
CW = 256
NCH = D // CW

TM_IN = 512
ROWS_PIECE = 512
L_MIX = 256
ROWS_MIX = 2
TM_FFN = 512
FFN_CHUNK = 1024
ADA_COLS = 1536

F32 = jnp.float32
BF16 = jnp.bfloat16
VMEM_LIMIT = 56 * 1024 * 1024


def _resident(shape):
    nd = len(shape)
    return pl.BlockSpec(shape, lambda *_: (0,) * nd, pipeline_mode=pl.Buffered(1))


def _ada_kernel(c_ref, w_ref, b_ref, o_ref):
    c = c_ref[...]
    ca = c * jax.nn.sigmoid(c)
    o_ref[...] = jnp.dot(ca, w_ref[...], preferred_element_type=F32,
                         precision=lax.Precision.HIGHEST) + b_ref[...]


def _ada(c, w, b):
    n = w.shape[1]
    return pl.pallas_call(
        _ada_kernel,
        grid=(n // ADA_COLS,),
        in_specs=[pl.BlockSpec((BATCH, D), lambda j: (0, 0)),
                  pl.BlockSpec((D, ADA_COLS), lambda j: (0, j)),
                  pl.BlockSpec((1, ADA_COLS), lambda j: (0, j))],
        out_specs=pl.BlockSpec((BATCH, ADA_COLS), lambda j: (0, j)),
        out_shape=jax.ShapeDtypeStruct((BATCH, n), F32),
        compiler_params=pltpu.CompilerParams(vmem_limit_bytes=VMEM_LIMIT),
        name="ada",
    )(c, w, b.reshape(1, n))


def _wprep_kernel(a_ref, b_ref, o_ref):
    s = pl.program_id(0)

    @pl.when(s < SEG_GA)
    def _():
        o_ref[...] = a_ref[...].T.astype(BF16)

    @pl.when(s >= SEG_GA)
    def _():
        rows = jnp.concatenate([a_ref[2 * HEADS:, :], b_ref[...]], axis=0)
        o_ref[...] = rows.T.astype(BF16)


def _wprep(wt):
    gate_rows = 2 * HEADS
    return pl.pallas_call(
        _wprep_kernel,
        grid=(N_SEG, NCH),
        in_specs=[pl.BlockSpec((CW, D), lambda s, c: (s * NCH + c, 0)),
                  pl.BlockSpec((gate_rows, D),
                               lambda s, c: ((s * NCH + c + 1) * (CW // gate_rows), 0))],
        out_specs=pl.BlockSpec((None, None, D, CW), lambda s, c: (s, c, 0, 0)),
        out_shape=jax.ShapeDtypeStruct((N_SEG, NCH, D, CW), BF16),
        compiler_params=pltpu.CompilerParams(vmem_limit_bytes=VMEM_LIMIT),
        name="wprep",
    )(wt, wt)


def _rms_mod(x, g, shift, scale):
    y = x * lax.rsqrt(jnp.mean(x * x, axis=-1, keepdims=True) + EPS)
    return (y * g) * (1.0 + scale) + shift


def _conv_silu(z, halo_ref, c, w, b):
    tm = z.shape[0]
    zz = jnp.concatenate([halo_ref[c], z], axis=0)
    acc = b + w[CONV_K - 1:CONV_K, :] * z
    for j in range(CONV_K - 1):
        shifted = pltpu.roll(zz, CONV_K - 1 - j, axis=0)[SUBLANES:, :]
        acc = acc + w[j:j + 1, :] * shifted
    halo_ref[c] = z[tm - SUBLANES:, :]
    return acc * jax.nn.sigmoid(acc)


def _inproj_kernel(x_ref, sh_ref, sc_ref, g_ref, w_ref, wg_ref, gb_ref, cw_ref, cb_ref,
                   lng_ref, lnb_ref,
                   u_ref, v_ref, q_ref, kt_ref, mv_ref, o_ref, ga_ref, gbo_ref, gate_ref,
                   xn_scr, gv_scr, s1_scr, halo_scr):
    i = pl.program_id(0)

    @pl.when((i % (SEQ // TM_IN)) == 0)
    def _():
        halo_scr[...] = jnp.zeros(halo_scr.shape, F32)

    xn_scr[...] = _rms_mod(x_ref[...], g_ref[...], sh_ref[...], sc_ref[...]).astype(BF16)
    s1_scr[...] = jnp.zeros(s1_scr.shape, F32)

    def seg(k, c, rs):
        return jnp.dot(xn_scr[rs, :], w_ref[k, c], preferred_element_type=F32)

    pieces = [slice(p * ROWS_PIECE, (p + 1) * ROWS_PIECE) for p in range(TM_IN // ROWS_PIECE)]

    def loop_main(c, carry):
        for rs in pieces:
            k = _conv_silu(seg(SEG_K, c, rs), halo_scr.at[1], c, cw_ref[1, c], cb_ref[1, c])
            kt_ref[c, :, rs] = k.T.astype(BF16)
            gv = jax.nn.gelu(seg(SEG_V, c, rs))
            gv_scr[c, rs, :] = gv
            s1_scr[rs, :] += gv[:, :LANES] + gv[:, LANES:]
            q = _conv_silu(seg(SEG_Q, c, rs), halo_scr.at[0], c, cw_ref[0, c], cb_ref[0, c])
            q_ref[c, rs, :] = (q * (HEAD_DIM ** -0.5)).astype(BF16)
            u_ref[c, rs, :] = jax.nn.gelu(seg(SEG_U, c, rs)).astype(BF16)
            o_ref[c, rs, :] = jax.nn.sigmoid(seg(SEG_O, c, rs)).astype(BF16)
            ga_ref[c, rs, :] = jax.nn.sigmoid(seg(SEG_GA, c, rs)).astype(BF16)
            gbo_ref[c, rs, :] = jax.nn.sigmoid(seg(SEG_GB, c, rs)).astype(BF16)
            mv_ref[c, rs, :] = seg(SEG_MV, c, rs).astype(BF16)
        return carry

    lax.fori_loop(0, NCH, loop_main, 0)

    wg = jnp.concatenate([wg_ref[...], jnp.zeros((LANES - 2 * HEADS, D), F32)], axis=0)
    gp = lax.dot_general(xn_scr[...], wg.astype(BF16), (((1,), (1,)), ((), ())),
                         preferred_element_type=F32) + gb_ref[...]
    lane = lax.broadcasted_iota(jnp.int32, gp.shape, 1)
    gate_ref[...] = jnp.where(lane < HEADS, gp, jax.nn.log_sigmoid(gp))

    mu = jnp.sum(s1_scr[...], axis=-1, keepdims=True) * (1.0 / D)
    ssq = jnp.zeros((TM_IN, LANES), F32)
    for c in range(NCH):
        dv = gv_scr[c] - mu
        dv = dv * dv
        ssq = ssq + (dv[:, :LANES] + dv[:, LANES:])
    rstd = lax.rsqrt(jnp.sum(ssq, axis=-1, keepdims=True) * (1.0 / D) + EPS)
    for c in range(NCH):
        v_ref[c] = (((gv_scr[c] - mu) * rstd) * lng_ref[c] + lnb_ref[c]).astype(BF16)


def _inproj(x2, sh1, sc1, norm_g, w_all, w_gate, gate_bias, conv_w, conv_b, ln_g, ln_b):
    nsb = SEQ // TM_IN
    row = lambda i: (i, 0)
    per_batch = lambda i: (i // nsb, 0, 0)
    seg_out = pl.BlockSpec((NCH, TM_IN, CW), lambda i: (0, i, 0))
    seg_shape = jax.ShapeDtypeStruct((NCH, TOKENS, CW), BF16)
    seg_outs = [seg_out] * N_SEG
    seg_shapes = [seg_shape] * N_SEG
    seg_outs[SEG_K] = pl.BlockSpec((NCH, CW, TM_IN), lambda i: (0, 0, i))
    seg_shapes[SEG_K] = jax.ShapeDtypeStruct((NCH, CW, TOKENS), BF16)
    return pl.pallas_call(
        _inproj_kernel,
        grid=(TOKENS // TM_IN,),
        in_specs=[pl.BlockSpec((TM_IN, D), row),
                  pl.BlockSpec((None, 1, D), per_batch),
                  pl.BlockSpec((None, 1, D), per_batch),
                  _resident((1, D)),
                  _resident((N_SEG, NCH, D, CW)),
                  _resident((2 * HEADS, D)),
                  _resident((1, LANES)),
                  _resident((2, NCH, CONV_K, CW)),
                  _resident((2, NCH, 1, CW)),
                  _resident((NCH, 1, CW)),
                  _resident((NCH, 1, CW))],
        out_specs=seg_outs + [pl.BlockSpec((TM_IN, LANES), row)],
        out_shape=seg_shapes + [jax.ShapeDtypeStruct((TOKENS, LANES), F32)],
        scratch_shapes=[pltpu.VMEM((TM_IN, D), BF16),
                        pltpu.VMEM((NCH, TM_IN, CW), F32),
                        pltpu.VMEM((TM_IN, LANES), F32),
                        pltpu.VMEM((2, NCH, SUBLANES, CW), F32)],
        compiler_params=pltpu.CompilerParams(dimension_semantics=("arbitrary",),
                                             vmem_limit_bytes=VMEM_LIMIT),
        name="inproj",
    )(x2, sh1, sc1, norm_g, w_all, w_gate, gate_bias, conv_w, conv_b, ln_g, ln_b)


def _cumsum_rows(x):
    n = x.shape[0]
    row = lax.broadcasted_iota(jnp.int32, x.shape, 0)
    k = 1
    while k < n:
        x = x + jnp.where(row >= k, pltpu.roll(x, k, axis=0), 0.0)
        k *= 2
    return x


def _rep2(a):
    return jnp.concatenate([a, a], axis=1)


def _mixer_kernel(x_ref, g1_ref, u_ref, v_ref, q_ref, kt0_ref, kt1_ref, mv_ref, o_ref, ga_ref,
                  gb_ref, gate_ref, ws_ref, bsb_ref, hng_ref, wout_ref,
                  h1_ref,
                  c_scr, n_scr, m_scr, merged_scr):
    L = L_MIX
    j = pl.program_id(1)

    @pl.when(j == 0)
    def _():
        c_scr[...] = jnp.zeros(c_scr.shape, F32)
        n_scr[...] = jnp.zeros(n_scr.shape, F32)
        m_scr[...] = jnp.zeros(m_scr.shape, F32)

    kt_refs = (kt0_ref, kt1_ref)
    scans = []
    for r in range(ROWS_MIX):
        gates = gate_ref[r]
        bcum = _cumsum_rows(gates)
        scans.append((gates.T, bcum, bcum.T))
    row = lax.broadcasted_iota(jnp.int32, (L, L), 0)
    col = lax.broadcasted_iota(jnp.int32, (L, L), 1)
    causal = col <= row
    ones_rhs = jnp.ones((L, LANES), BF16)

    brow = lax.broadcasted_iota(jnp.int32, (GMLP_BLOCK, GMLP_BLOCK), 0) // 64
    bcol = lax.broadcasted_iota(jnp.int32, (GMLP_BLOCK, GMLP_BLOCK), 1) // 64
    wmask = bcol <= brow
    nblk = L // GMLP_BLOCK

    for h in range(HEADS):
        hs = slice(h * HEAD_DIM, (h + 1) * HEAD_DIM)
        for r in range(ROWS_MIX):
            gates_t, bcum, bcum_t = scans[r]
            li_r = gates_t[h:h + 1, :]
            b_r = bcum_t[HEADS + h:HEADS + h + 1, :]
            b_c = jnp.broadcast_to(bcum[:, HEADS + h:HEADS + h + 1], (L, LANES))
            m_prev = m_scr[r, h, 0:1, 0:1]
            dmat = jnp.where(causal, _rep2(b_c) + (li_r - b_r), -jnp.inf)
            m_inter = b_c + m_prev
            m_t = jnp.maximum(m_inter, jnp.max(dmat, axis=-1, keepdims=True))
            wts = jnp.exp(dmat - _rep2(m_t))
            qh = q_ref[h, r]
            kt = kt_refs[r][h]
            vh = mv_ref[h, r]
            s = jnp.dot(qh, kt, preferred_element_type=F32) * wts
            sb = s.astype(BF16)
            inter = jnp.exp(m_inter - m_t)
            c_old = c_scr[r, h]
            n_old = n_scr[r, h]
            n_hi = n_old.astype(BF16)
            n_lo = (n_old - n_hi.astype(F32)).astype(BF16)
            qn2 = jnp.dot(qh, jnp.concatenate([n_hi, n_lo], axis=1), preferred_element_type=F32)
            qn = qn2[:, :LANES] + qn2[:, LANES:]
            num = (jnp.dot(sb, vh, preferred_element_type=F32)
                   + _rep2(inter) * jnp.dot(qh, c_old.astype(BF16), preferred_element_type=F32))
            den = jnp.dot(sb, ones_rhs, preferred_element_type=F32) + inter * qn
            inv = 1.0 / jnp.maximum(jnp.abs(den), jnp.exp(-m_t))
            hh = num * _rep2(inv)

            b_last = b_r[:, L - 1:L]
            g_r = (b_last - b_r) + li_r
            m_new = jnp.maximum(b_last + m_prev, jnp.max(g_r, axis=-1, keepdims=True))
            decay = jnp.exp((b_last + m_prev) - m_new)
            wkt = jnp.exp(g_r - m_new).astype(BF16) * kt
            c_scr[r, h] = decay * c_old + jnp.dot(wkt, vh, preferred_element_type=F32)
            n_scr[r, h] = decay * n_old + jnp.dot(wkt, ones_rhs, preferred_element_type=F32)
            m_scr[r, h] = jnp.broadcast_to(m_new, (SUBLANES, LANES))

            mu = jnp.mean(hh, axis=-1, keepdims=True)
            dh = hh - mu
            var = jnp.mean(dh * dh, axis=-1, keepdims=True)
            y_n = (dh * lax.rsqrt(var + EPS)) * hng_ref[:, hs]

            for half in range(CW // GROUP_DIM):
                g = h * (CW // GROUP_DIM) + half
                ls = slice(half * GROUP_DIM, (half + 1) * GROUP_DIM)
                wsg = jnp.where(wmask, ws_ref[g], 0.0).astype(BF16)
                vcat = jnp.concatenate(
                    [v_ref[h, r, n * GMLP_BLOCK:(n + 1) * GMLP_BLOCK, ls] for n in range(nblk)],
                    axis=1)
                mixed = jnp.dot(wsg, vcat, preferred_element_type=F32) + _rep2(bsb_ref[g])
                for n in range(nblk):
                    rs = slice(n * GMLP_BLOCK, (n + 1) * GMLP_BLOCK)
                    gate_a = (ga_ref[h, r, rs, ls] * u_ref[h, r, rs, ls]).astype(F32)
                    gate_b = (gb_ref[h, r, rs, ls] * o_ref[h, r, rs, ls]).astype(F32)
                    merged = (gate_a * mixed[:, n * GMLP_BLOCK:(n + 1) * GMLP_BLOCK]
                              + gate_b * y_n[rs, ls])
                    merged_scr[r * L + n * GMLP_BLOCK:r * L + (n + 1) * GMLP_BLOCK,
                               g * GROUP_DIM:(g + 1) * GROUP_DIM] = merged.astype(BF16)

    proj = jnp.dot(merged_scr[...], wout_ref[...], preferred_element_type=F32)
    for r in range(ROWS_MIX):
        h1_ref[r] = x_ref[r] + g1_ref[r] * proj[r * L:(r + 1) * L, :]


def _mixer(x3, g1, u, v, q, kt, mv, o, ga, gb, gates, ws, bs_rep, hn_g, w_out):
    nsb = SEQ // L_MIX
    rows3 = pl.BlockSpec((ROWS_MIX, L_MIX, D), lambda b, j: (b, j, 0))
    seg_in = pl.BlockSpec((NCH, ROWS_MIX, L_MIX, CW), lambda b, j: (0, b, j, 0))
    kt_in = [pl.BlockSpec((NCH, CW, L_MIX), lambda b, j, r=r: (0, 0, (ROWS_MIX * b + r) * nsb + j))
             for r in range(ROWS_MIX)]
    seg4 = lambda a: a.reshape(NCH, BATCH, SEQ, CW)
    return pl.pallas_call(
        _mixer_kernel,
        grid=(BATCH // ROWS_MIX, nsb),
        in_specs=[rows3,
                  pl.BlockSpec((ROWS_MIX, 1, D), lambda b, j: (b, 0, 0)),
                  seg_in, seg_in, seg_in, kt_in[0], kt_in[1], seg_in, seg_in, seg_in, seg_in,
                  pl.BlockSpec((ROWS_MIX, L_MIX, LANES), lambda b, j: (b, j, 0)),
                  _resident((GMLP_GROUPS, GMLP_BLOCK, GMLP_BLOCK)),
                  _resident((GMLP_GROUPS, GMLP_BLOCK, LANES)),
                  _resident((1, D)),
                  _resident((D, D))],
        out_specs=rows3,
        out_shape=jax.ShapeDtypeStruct((BATCH, SEQ, D), F32),
        scratch_shapes=[pltpu.VMEM((ROWS_MIX, HEADS, HEAD_DIM, HEAD_DIM), F32),
                        pltpu.VMEM((ROWS_MIX, HEADS, HEAD_DIM, LANES), F32),
                        pltpu.VMEM((ROWS_MIX, HEADS, SUBLANES, LANES), F32),
                        pltpu.VMEM((ROWS_MIX * L_MIX, D), BF16)],
        compiler_params=pltpu.CompilerParams(dimension_semantics=("arbitrary", "arbitrary"),
                                             vmem_limit_bytes=VMEM_LIMIT),
        name="mixer",
    )(x3, g1, seg4(u), seg4(v), seg4(q), kt, kt, seg4(mv), seg4(o), seg4(ga), seg4(gb),
      gates.reshape(BATCH, SEQ, LANES), ws, bs_rep, hn_g, w_out)


def _ffn_kernel(h_ref, sh_ref, sc_ref, g2_ref, ng_ref, w1_ref, w2_ref, fg_ref, o_ref):
    h = h_ref[...]
    xn = _rms_mod(h, ng_ref[...], sh_ref[...], sc_ref[...]).astype(BF16)
    acc = jnp.zeros((TM_FFN, D), F32)
    for c in range(FFN // FFN_CHUNK):
        cs = slice(c * FFN_CHUNK, (c + 1) * FFN_CHUNK)
        a = jnp.maximum(jnp.dot(xn, w1_ref[:, cs], preferred_element_type=F32), 0.0)
        acc = acc + jnp.dot((a * a).astype(BF16), w2_ref[cs, :], preferred_element_type=F32)
    h2 = h + g2_ref[...] * acc
    y = h2 * lax.rsqrt(jnp.mean(h2 * h2, axis=-1, keepdims=True) + EPS)
    o_ref[...] = y * fg_ref[...]


def _ffn(h1, sh2, sc2, g2, norm_g, w1, w2, final_g):
    nsb = SEQ // TM_FFN
    row = lambda i: (i, 0)
    per_batch = lambda i: (i // nsb, 0, 0)
    return pl.pallas_call(
        _ffn_kernel,
        grid=(TOKENS // TM_FFN,),
        in_specs=[pl.BlockSpec((TM_FFN, D), row),
                  pl.BlockSpec((None, 1, D), per_batch),
                  pl.BlockSpec((None, 1, D), per_batch),
                  pl.BlockSpec((None, 1, D), per_batch),
                  _resident((1, D)),
                  _resident((D, FFN)),
                  _resident((FFN, D)),
                  _resident((1, D))],
        out_specs=pl.BlockSpec((TM_FFN, D), row),
        out_shape=jax.ShapeDtypeStruct((TOKENS, D), F32),
        compiler_params=pltpu.CompilerParams(dimension_semantics=("arbitrary",),
                                             vmem_limit_bytes=VMEM_LIMIT),
        name="ffn",
    )(h1, sh2, sc2, g2, norm_g, w1, w2, final_g)


def kernel(x, c, w_ada, b_ada, norm1_g, w_in, conv_w, conv_b, mlstm_gate_b, gmlp_ln_g, gmlp_ln_b,
           gmlp_ws, gmlp_bs, mlstm_hn_g, w_out, norm2_g, w_ff1, w_ff2, final_g):
    l = 0
    mod = _ada(c, w_ada[l], b_ada[l]).reshape(BATCH, 6, 1, D)
    sh1, sc1, g1, sh2, sc2, g2 = (mod[:, t] for t in range(6))

    wt = jnp.swapaxes(w_in, 1, 2)[l]
    w_all = _wprep(wt)
    w_gate = wt[OFF_I:OFF_GA]
    gate_bias = jnp.pad(mlstm_gate_b[l].reshape(1, 2 * HEADS), ((0, 0), (0, LANES - 2 * HEADS)))
    cw = conv_w[l].reshape(CONV_K, 2, NCH, CW).transpose(1, 2, 0, 3)
    cb = conv_b[l].reshape(2, NCH, 1, CW)

    x2 = x.reshape(TOKENS, D)
    u, v, q, kt, mv, o, ga, gb, gates = _inproj(
        x2, sh1, sc1, norm1_g[l].reshape(1, D), w_all, w_gate, gate_bias, cw, cb,
        gmlp_ln_g[l].reshape(NCH, 1, CW), gmlp_ln_b[l].reshape(NCH, 1, CW))

    bs_rep = jnp.broadcast_to(gmlp_bs[l][:, :, None], (GMLP_GROUPS, GMLP_BLOCK, LANES))
    h1 = _mixer(x, g1, u, v, q, kt, mv, o, ga, gb, gates,
                gmlp_ws[l], bs_rep, mlstm_hn_g[l].reshape(1, D), w_out[l].astype(BF16))
    h1 = h1.reshape(TOKENS, D)

    out = _ffn(h1, sh2, sc2, g2, norm2_g[l].reshape(1, D),
               w_ff1[l].astype(BF16), w_ff2[l].astype(BF16), final_g.reshape(1, D))
    return out.reshape(BATCH, SEQ, D)
```

```python
import jax
import jax.numpy as jnp
from jax import lax
from jax.experimental import pallas as pl
from jax.experimental.pallas import tpu as pltpu

D = 1024
BATCH = 16
SEQ = 2048
TOKENS = BATCH * SEQ
CHUNK = 64
GMLP_BLOCK = 128
GMLP_GROUPS = 8
GROUP_DIM = D // GMLP_GROUPS
HEADS = 4
HEAD_DIM = 256
CONV_K = 4
FFN = 4 * D
EPS = 1e-6
LANES = 128
SUBLANES = 8

OFF_I = 6 * D
OFF_GA = OFF_I + 2 * HEADS
N_SEG = 8
SEG_U, SEG_V, SEG_Q, SEG_K, SEG_MV, SEG_O, SEG_GA, SEG_GB = range(N_SEG)
CW = 256
NCH = D // CW

TM_IN = 512
ROWS_PIECE = 512
L_MIX = 256
ROWS_MIX = 2
TM_FFN = 512
FFN_CHUNK = 1024
ADA_COLS = 1536

F32 = jnp.float32
BF16 = jnp.bfloat16
VMEM_LIMIT = 56 * 1024 * 1024


def _resident(shape):
    nd = len(shape)
    return pl.BlockSpec(shape, lambda *_: (0,) * nd, pipeline_mode=pl.Buffered(1))


def _ada_kernel(c_ref, w_ref, b_ref, o_ref):
    c = c_ref[...]
    ca = c * jax.nn.sigmoid(c)
    o_ref[...] = jnp.dot(ca, w_ref[...], preferred_element_type=F32,
                         precision=lax.Precision.HIGHEST) + b_ref[...]


def _ada(c, w, b):
    n = w.shape[1]
    return pl.pallas_call(
        _ada_kernel,
        grid=(n // ADA_COLS,),
        in_specs=[pl.BlockSpec((BATCH, D), lambda j: (0, 0)),
                  pl.BlockSpec((D, ADA_COLS), lambda j: (0, j)),
                  pl.BlockSpec((1, ADA_COLS), lambda j: (0, j))],
        out_specs=pl.BlockSpec((BATCH, ADA_COLS), lambda j: (0, j)),
        out_shape=jax.ShapeDtypeStruct((BATCH, n), F32),
        compiler_params=pltpu.CompilerParams(vmem_limit_bytes=VMEM_LIMIT),
        name="ada",
    )(c, w, b.reshape(1, n))


def _wprep_kernel(a_ref, b_ref, o_ref):
    s = pl.program_id(0)

    @pl.when(s < SEG_GA)
    def _():
        o_ref[...] = a_ref[...].T.astype(BF16)

    @pl.when(s >= SEG_GA)
    def _():
        rows = jnp.concatenate([a_ref[2 * HEADS:, :], b_ref[...]], axis=0)
        o_ref[...] = rows.T.astype(BF16)


def _wprep(wt):
    gate_rows = 2 * HEADS
    return pl.pallas_call(
        _wprep_kernel,
        grid=(N_SEG, NCH),
        in_specs=[pl.BlockSpec((CW, D), lambda s, c: (s * NCH + c, 0)),
                  pl.BlockSpec((gate_rows, D),
                               lambda s, c: ((s * NCH + c + 1) * (CW // gate_rows), 0))],
        out_specs=pl.BlockSpec((None, None, D, CW), lambda s, c: (s, c, 0, 0)),
        out_shape=jax.ShapeDtypeStruct((N_SEG, NCH, D, CW), BF16),
        compiler_params=pltpu.CompilerParams(vmem_limit_bytes=VMEM_LIMIT),
        name="wprep",
    )(wt, wt)


def _rms_mod(x, g, shift, scale):
    y = x * lax.rsqrt(jnp.mean(x * x, axis=-1, keepdims=True) + EPS)
    return (y * g) * (1.0 + scale) + shift


def _conv_silu(z, halo_ref, c, w, b):
    tm = z.shape[0]
    zz = jnp.concatenate([halo_ref[c], z], axis=0)
    acc = b + w[CONV_K - 1:CONV_K, :] * z
    for j in range(CONV_K - 1):
        shifted = pltpu.roll(zz, CONV_K - 1 - j, axis=0)[SUBLANES:, :]
        acc = acc + w[j:j + 1, :] * shifted
    halo_ref[c] = z[tm - SUBLANES:, :]
    return acc * jax.nn.sigmoid(acc)


def _inproj_kernel(x_ref, sh_ref, sc_ref, g_ref, w_ref, wg_ref, gb_ref, cw_ref, cb_ref,
                   lng_ref, lnb_ref,
                   u_ref, v_ref, q_ref, kt_ref, mv_ref, o_ref, ga_ref, gbo_ref, gate_ref,
                   xn_scr, gv_scr, s1_scr, halo_scr):
    i = pl.program_id(0)

    @pl.when((i % (SEQ // TM_IN)) == 0)
    def _():
        halo_scr[...] = jnp.zeros(halo_scr.shape, F32)

    xn_scr[...] = _rms_mod(x_ref[...], g_ref[...], sh_ref[...], sc_ref[...]).astype(BF16)
    s1_scr[...] = jnp.zeros(s1_scr.shape, F32)

    def seg(k, c, rs):
        return jnp.dot(xn_scr[rs, :], w_ref[k, c], preferred_element_type=F32)

    pieces = [slice(p * ROWS_PIECE, (p + 1) * ROWS_PIECE) for p in range(TM_IN // ROWS_PIECE)]

    def loop_main(c, carry):
        for rs in pieces:
            k = _conv_silu(seg(SEG_K, c, rs), halo_scr.at[1], c, cw_ref[1, c], cb_ref[1, c])
            kt_ref[c, :, rs] = k.T.astype(BF16)
            mv_ref[c, rs, :] = seg(SEG_MV, c, rs).astype(BF16)
            q = _conv_silu(seg(SEG_Q, c, rs), halo_scr.at[0], c, cw_ref[0, c], cb_ref[0, c])
            q_ref[c, rs, :] = (q * (HEAD_DIM ** -0.5)).astype(BF16)
            o_ref[c, rs, :] = jax.nn.sigmoid(seg(SEG_O, c, rs)).astype(BF16)
            gv = jax.nn.gelu(seg(SEG_V, c, rs))
            gv_scr[c, rs, :] = gv
            s1_scr[rs, :] += gv[:, :LANES] + gv[:, LANES:]
            ga_ref[c, rs, :] = jax.nn.sigmoid(seg(SEG_GA, c, rs)).astype(BF16)
            u_ref[c, rs, :] = jax.nn.gelu(seg(SEG_U, c, rs)).astype(BF16)
            gbo_ref[c, rs, :] = jax.nn.sigmoid(seg(SEG_GB, c, rs)).astype(BF16)
        return carry

    lax.fori_loop(0, NCH, loop_main, 0)

    wg = jnp.concatenate([wg_ref[...], jnp.zeros((LANES - 2 * HEADS, D), F32)], axis=0)
    gp = lax.dot_general(xn_scr[...], wg.astype(BF16), (((1,), (1,)), ((), ())),
                         preferred_element_type=F32) + gb_ref[...]
    lane = lax.broadcasted_iota(jnp.int32, gp.shape, 1)
    gate_ref[...] = jnp.where(lane < HEADS, gp, jax.nn.log_sigmoid(gp))

    mu = jnp.sum(s1_scr[...], axis=-1, keepdims=True) * (1.0 / D)
    ssq = jnp.zeros((TM_IN, LANES), F32)
    for c in range(NCH):
        dv = gv_scr[c] - mu
        dv = dv * dv
        ssq = ssq + (dv[:, :LANES] + dv[:, LANES:])
    rstd = lax.rsqrt(jnp.sum(ssq, axis=-1, keepdims=True) * (1.0 / D) + EPS)
    for c in range(NCH):
        v_ref[c] = (((gv_scr[c] - mu) * rstd) * lng_ref[c] + lnb_ref[c]).astype(BF16)


def _inproj(x2, sh1, sc1, norm_g, w_all, w_gate, gate_bias, conv_w, conv_b, ln_g, ln_b):
    nsb = SEQ // TM_IN
    row = lambda i: (i, 0)
    per_batch = lambda i: (i // nsb, 0, 0)
    seg_out = pl.BlockSpec((NCH, TM_IN, CW), lambda i: (0, i, 0))
    seg_shape = jax.ShapeDtypeStruct((NCH, TOKENS, CW), BF16)
    seg_outs = [seg_out] * N_SEG
    seg_shapes = [seg_shape] * N_SEG
    seg_outs[SEG_K] = pl.BlockSpec((NCH, CW, TM_IN), lambda i: (0, 0, i))
    seg_shapes[SEG_K] = jax.ShapeDtypeStruct((NCH, CW, TOKENS), BF16)
    return pl.pallas_call(
        _inproj_kernel,
        grid=(TOKENS // TM_IN,),
        in_specs=[pl.BlockSpec((TM_IN, D), row),
                  pl.BlockSpec((None, 1, D), per_batch),
                  pl.BlockSpec((None, 1, D), per_batch),
                  _resident((1, D)),
                  _resident((N_SEG, NCH, D, CW)),
                  _resident((2 * HEADS, D)),
                  _resident((1, LANES)),
                  _resident((2, NCH, CONV_K, CW)),
                  _resident((2, NCH, 1, CW)),
                  _resident((NCH, 1, CW)),
                  _resident((NCH, 1, CW))],
        out_specs=seg_outs + [pl.BlockSpec((TM_IN, LANES), row)],
        out_shape=seg_shapes + [jax.ShapeDtypeStruct((TOKENS, LANES), F32)],
        scratch_shapes=[pltpu.VMEM((TM_IN, D), BF16),
                        pltpu.VMEM((NCH, TM_IN, CW), F32),
                        pltpu.VMEM((TM_IN, LANES), F32),
                        pltpu.VMEM((2, NCH, SUBLANES, CW), F32)],
        compiler_params=pltpu.CompilerParams(dimension_semantics=("arbitrary",),
                                             vmem_limit_bytes=VMEM_LIMIT),
        name="inproj",
    )(x2, sh1, sc1, norm_g, w_all, w_gate, gate_bias, conv_w, conv_b, ln_g, ln_b)


def _cumsum_rows(x):
    n = x.shape[0]
    row = lax.broadcasted_iota(jnp.int32, x.shape, 0)
    k = 1
    while k < n:
        x = x + jnp.where(row >= k, pltpu.roll(x, k, axis=0), 0.0)
        k *= 2
    return x


def _rep2(a):
    return jnp.concatenate([a, a], axis=1)


def _mixer_kernel(x_ref, g1_ref, u_ref, v_ref, q_ref, kt0_ref, kt1_ref, mv_ref, o_ref, ga_ref,
                  gb_ref, gate_ref, ws_ref, bsb_ref, hng_ref, wout_ref,
                  h1_ref,
                  c_scr, n_scr, m_scr, merged_scr):
    L = L_MIX
    j = pl.program_id(1)

    @pl.when(j == 0)
    def _():
        c_scr[...] = jnp.zeros(c_scr.shape, F32)
        n_scr[...] = jnp.zeros(n_scr.shape, F32)
        m_scr[...] = jnp.zeros(m_scr.shape, F32)

    kt_refs = (kt0_ref, kt1_ref)
    scans = []
    for r in range(ROWS_MIX):
        gates = gate_ref[r]
        bcum = _cumsum_rows(gates)
        scans.append((gates.T, bcum, bcum.T))
    row = lax.broadcasted_iota(jnp.int32, (L, L), 0)
    col = lax.broadcasted_iota(jnp.int32, (L, L), 1)
    causal = col <= row
    ones_rhs = jnp.ones((L, LANES), BF16)

    brow = lax.broadcasted_iota(jnp.int32, (GMLP_BLOCK, GMLP_BLOCK), 0) // CHUNK
    bcol = lax.broadcasted_iota(jnp.int32, (GMLP_BLOCK, GMLP_BLOCK), 1) // CHUNK
    wmask = bcol <= brow
    nblk = L // GMLP_BLOCK

    for h in range(HEADS):
        hs = slice(h * HEAD_DIM, (h + 1) * HEAD_DIM)
        for r in range(ROWS_MIX):
            gates_t, bcum, bcum_t = scans[r]
            li_r = gates_t[h:h + 1, :]
            b_r = bcum_t[HEADS + h:HEADS + h + 1, :]
            b_c = jnp.broadcast_to(bcum[:, HEADS + h:HEADS + h + 1], (L, LANES))
            m_prev = m_scr[r, h, 0:1, 0:1]
            dmat = jnp.where(causal, _rep2(b_c) + (li_r - b_r), -jnp.inf)
            m_inter = b_c + m_prev
            m_t = jnp.maximum(m_inter, jnp.max(dmat, axis=-1, keepdims=True))
            wts = jnp.exp(dmat - _rep2(m_t))
            qh = q_ref[h, r]
            kt = kt_refs[r][h]
            vh = mv_ref[h, r]
            s = jnp.dot(qh, kt, preferred_element_type=F32) * wts
            sb = s.astype(BF16)
            inter = jnp.exp(m_inter - m_t)
            c_old = c_scr[r, h]
            n_old = n_scr[r, h]
            n_hi = n_old.astype(BF16)
            n_lo = (n_old - n_hi.astype(F32)).astype(BF16)
            qn2 = jnp.dot(qh, jnp.concatenate([n_hi, n_lo], axis=1), preferred_element_type=F32)
            qn = qn2[:, :LANES] + qn2[:, LANES:]
            num = (jnp.dot(sb, vh, preferred_element_type=F32)
                   + _rep2(inter) * jnp.dot(qh, c_old.astype(BF16), preferred_element_type=F32))
            den = jnp.dot(sb, ones_rhs, preferred_element_type=F32) + inter * qn
            inv = 1.0 / jnp.maximum(jnp.abs(den), jnp.exp(-m_t))
            hh = num * _rep2(inv)

            b_last = b_r[:, L - 1:L]
            g_r = (b_last - b_r) + li_r
            m_new = jnp.maximum(b_last + m_prev, jnp.max(g_r, axis=-1, keepdims=True))
            decay = jnp.exp((b_last + m_prev) - m_new)
            wkt = jnp.exp(g_r - m_new).astype(BF16) * kt
            c_scr[r, h] = decay * c_old + jnp.dot(wkt, vh, preferred_element_type=F32)
            n_scr[r, h] = decay * n_old + jnp.dot(wkt, ones_rhs, preferred_element_type=F32)
            m_scr[r, h] = jnp.broadcast_to(m_new, (SUBLANES, LANES))

            mu = jnp.mean(hh, axis=-1, keepdims=True)
            dh = hh - mu
            var = jnp.mean(dh * dh, axis=-1, keepdims=True)
            y_n = (dh * lax.rsqrt(var + EPS)) * hng_ref[:, hs]

            for half in range(CW // GROUP_DIM):
                g = h * (CW // GROUP_DIM) + half
                ls = slice(half * GROUP_DIM, (half + 1) * GROUP_DIM)
                wsg = jnp.where(wmask, ws_ref[g], 0.0).astype(BF16)
                vcat = jnp.concatenate(
                    [v_ref[h, r, n * GMLP_BLOCK:(n + 1) * GMLP_BLOCK, ls] for n in range(nblk)],
                    axis=1)
                mixed = jnp.dot(wsg, vcat, preferred_element_type=F32) + _rep2(bsb_ref[g])
                for n in range(nblk):
                    rs = slice(n * GMLP_BLOCK, (n + 1) * GMLP_BLOCK)
                    gate_a = (ga_ref[h, r, rs, ls] * u_ref[h, r, rs, ls]).astype(F32)
                    gate_b = (gb_ref[h, r, rs, ls] * o_ref[h, r, rs, ls]).astype(F32)
                    merged = (gate_a * mixed[:, n * GMLP_BLOCK:(n + 1) * GMLP_BLOCK]
                              + gate_b * y_n[rs, ls])
                    merged_scr[r * L + n * GMLP_BLOCK:r * L + (n + 1) * GMLP_BLOCK,
                               g * GROUP_DIM:(g + 1) * GROUP_DIM] = merged.astype(BF16)

    proj = jnp.dot(merged_scr[...], wout_ref[...], preferred_element_type=F32)
    for r in range(ROWS_MIX):
        h1_ref[r] = x_ref[r] + g1_ref[r] * proj[r * L:(r + 1) * L, :]


def _mixer(x3, g1, u, v, q, kt, mv, o, ga, gb, gates, ws, bs_rep, hn_g, w_out):
    nsb = SEQ // L_MIX
    rows3 = pl.BlockSpec((ROWS_MIX, L_MIX, D), lambda b, j: (b, j, 0))
    seg_in = pl.BlockSpec((NCH, ROWS_MIX, L_MIX, CW), lambda b, j: (0, b, j, 0))
    kt_in = [pl.BlockSpec((NCH, CW, L_MIX), lambda b, j, r=r: (0, 0, (ROWS_MIX * b + r) * nsb + j))
             for r in range(ROWS_MIX)]
    seg4 = lambda a: a.reshape(NCH, BATCH, SEQ, CW)
    return pl.pallas_call(
        _mixer_kernel,
        grid=(BATCH // ROWS_MIX, nsb),
        in_specs=[rows3,
                  pl.BlockSpec((ROWS_MIX, 1, D), lambda b, j: (b, 0, 0)),
                  seg_in, seg_in, seg_in, kt_in[0], kt_in[1], seg_in, seg_in, seg_in, seg_in,
                  pl.BlockSpec((ROWS_MIX, L_MIX, LANES), lambda b, j: (b, j, 0)),
                  _resident((GMLP_GROUPS, GMLP_BLOCK, GMLP_BLOCK)),
                  _resident((GMLP_GROUPS, GMLP_BLOCK, LANES)),
                  _resident((1, D)),
                  _resident((D, D))],
        out_specs=rows3,
        out_shape=jax.ShapeDtypeStruct((BATCH, SEQ, D), F32),
        scratch_shapes=[pltpu.VMEM((ROWS_MIX, HEADS, HEAD_DIM, HEAD_DIM), F32),
                        pltpu.VMEM((ROWS_MIX, HEADS, HEAD_DIM, LANES), F32),
                        pltpu.VMEM((ROWS_MIX, HEADS, SUBLANES, LANES), F32),
                        pltpu.VMEM((ROWS_MIX * L_MIX, D), BF16)],
        compiler_params=pltpu.CompilerParams(dimension_semantics=("arbitrary", "arbitrary"),
                                             vmem_limit_bytes=VMEM_LIMIT),
        name="mixer",
    )(x3, g1, seg4(u), seg4(v), seg4(q), kt, kt, seg4(mv), seg4(o), seg4(ga), seg4(gb),
      gates.reshape(BATCH, SEQ, LANES), ws, bs_rep, hn_g, w_out)


def _ffn_kernel(h_ref, sh_ref, sc_ref, g2_ref, ng_ref, w1_ref, w2_ref, fg_ref, o_ref):
    h = h_ref[...]
    xn = _rms_mod(h, ng_ref[...], sh_ref[...], sc_ref[...]).astype(BF16)
    acc = jnp.zeros((TM_FFN, D), F32)
    for c in range(FFN // FFN_CHUNK):
        cs = slice(c * FFN_CHUNK, (c + 1) * FFN_CHUNK)
        a = jnp.maximum(jnp.dot(xn, w1_ref[:, cs], preferred_element_type=F32), 0.0)
        acc = acc + jnp.dot((a * a).astype(BF16), w2_ref[cs, :], preferred_element_type=F32)
    h2 = h + g2_ref[...] * acc
    y = h2 * lax.rsqrt(jnp.mean(h2 * h2, axis=-1, keepdims=True) + EPS)
    o_ref[...] = y * fg_ref[...]


def _ffn(h1, sh2, sc2, g2, norm_g, w1, w2, final_g):
    nsb = SEQ // TM_FFN
    row = lambda i: (i, 0)
    per_batch = lambda i: (i // nsb, 0, 0)
    return pl.pallas_call(
        _ffn_kernel,
        grid=(TOKENS // TM_FFN,),
        in_specs=[pl.BlockSpec((TM_FFN, D), row),
                  pl.BlockSpec((None, 1, D), per_batch),
                  pl.BlockSpec((None, 1, D), per_batch),
                  pl.BlockSpec((None, 1, D), per_batch),
                  _resident((1, D)),
                  _resident((D, FFN)),
                  _resident((FFN, D)),
                  _resident((1, D))],
        out_specs=pl.BlockSpec((TM_FFN, D), row),
        out_shape=jax.ShapeDtypeStruct((TOKENS, D), F32),
        compiler_params=pltpu.CompilerParams(dimension_semantics=("arbitrary",),
                                             vmem_limit_bytes=VMEM_LIMIT),
        name="ffn",
    )(h1, sh2, sc2, g2, norm_g, w1, w2, final_g)


def kernel(x, c, w_ada, b_ada, norm1_g, w_in, conv_w, conv_b, mlstm_gate_b, gmlp_ln_g, gmlp_ln_b,
           gmlp_ws, gmlp_bs, mlstm_hn_g, w_out, norm2_g, w_ff1, w_ff2, final_g):
    l = 0
    mod = _ada(c, w_ada[l], b_ada[l]).reshape(BATCH, 6, 1, D)
    sh1, sc1, g1, sh2, sc2, g2 = (mod[:, t] for t in range(6))

    wt = jnp.swapaxes(w_in, 1, 2)[l]
    w_all = _wprep(wt)
    w_gate = wt[OFF_I:OFF_GA]
    gate_bias = jnp.pad(mlstm_gate_b[l].reshape(1, 2 * HEADS), ((0, 0), (0, LANES - 2 * HEADS)))
    cw = conv_w[l].reshape(CONV_K, 2, NCH, CW).transpose(1, 2, 0, 3)
    cb = conv_b[l].reshape(2, NCH, 1, CW)

    x2 = x.reshape(TOKENS, D)
    u, v, q, kt, mv, o, ga, gb, gates = _inproj(
        x2, sh1, sc1, norm1_g[l].reshape(1, D), w_all, w_gate, gate_bias, cw, cb,
        gmlp_ln_g[l].reshape(NCH, 1, CW), gmlp_ln_b[l].reshape(NCH, 1, CW))

    bs_rep = jnp.broadcast_to(gmlp_bs[l][:, :, None], (GMLP_GROUPS, GMLP_BLOCK, LANES))
    h1 = _mixer(x, g1, u, v, q, kt, mv, o, ga, gb, gates,
                gmlp_ws[l], bs_rep, mlstm_hn_g[l].reshape(1, D), w_out[l].astype(BF16))
    h1 = h1.reshape(TOKENS, D)

    out = _ffn(h1, sh2, sc2, g2, norm2_g[l].reshape(1, D),
               w_ff1[l].astype(BF16), w_ff2[l].astype(BF16), final_g.reshape(1, D))
    return out.reshape(BATCH, SEQ, D)
```

```python
import jax
import jax.numpy as jnp
from jax import lax
from jax.experimental import pallas as pl
from jax.experimental.pallas import tpu as pltpu

D = 1024
BATCH = 16
SEQ = 2048
TOKENS = BATCH * SEQ
CHUNK = 64
GMLP_BLOCK = 128
GMLP_GROUPS = 8
GROUP_DIM = D // GMLP_GROUPS
HEADS = 4
HEAD_DIM = 256
CONV_K = 4
FFN = 4 * D
EPS = 1e-6
LANES = 128
SUBLANES = 8

OFF_I = 6 * D
OFF_GA = OFF_I + 2 * HEADS
N_SEG = 8
SEG_U, SEG_V, SEG_Q, SEG_K, SEG_MV, SEG_O, SEG_GA, SEG_GB = range(N_SEG)
CW = 256
NCH = D // CW

TM_IN = 512
L_MIX = 256
ROWS_MIX = 2
TM_FFN = 512
FFN_CHUNK = 1024
ADA_COLS = 1536

F32 = jnp.float32
BF16 = jnp.bfloat16
VMEM_LIMIT = 56 * 1024 * 1024


def _resident(shape):
    nd = len(shape)
    return pl.BlockSpec(shape, lambda *_: (0,) * nd, pipeline_mode=pl.Buffered(1))


def _ada_kernel(c_ref, w_ref, b_ref, o_ref):
    c = c_ref[...]
    ca = c * jax.nn.sigmoid(c)
    o_ref[...] = jnp.dot(ca, w_ref[...], preferred_element_type=F32,
                         precision=lax.Precision.HIGHEST) + b_ref[...]


def _ada(c, w, b):
    n = w.shape[1]
    return pl.pallas_call(
        _ada_kernel,
        grid=(n // ADA_COLS,),
        in_specs=[pl.BlockSpec((BATCH, D), lambda j: (0, 0)),
                  pl.BlockSpec((D, ADA_COLS), lambda j: (0, j)),
                  pl.BlockSpec((1, ADA_COLS), lambda j: (0, j))],
        out_specs=pl.BlockSpec((BATCH, ADA_COLS), lambda j: (0, j)),
        out_shape=jax.ShapeDtypeStruct((BATCH, n), F32),
        compiler_params=pltpu.CompilerParams(vmem_limit_bytes=VMEM_LIMIT),
        name="ada",
    )(c, w, b.reshape(1, n))


def _wprep_kernel(a_ref, b_ref, o_ref):
    s = pl.program_id(0)

    @pl.when(s < SEG_GA)
    def _():
        o_ref[...] = a_ref[...].T.astype(BF16)

    @pl.when(s >= SEG_GA)
    def _():
        rows = jnp.concatenate([a_ref[2 * HEADS:, :], b_ref[...]], axis=0)
        o_ref[...] = rows.T.astype(BF16)


def _wprep(wt):
    gate_rows = 2 * HEADS
    return pl.pallas_call(
        _wprep_kernel,
        grid=(N_SEG, NCH),
        in_specs=[pl.BlockSpec((CW, D), lambda s, c: (s * NCH + c, 0)),
                  pl.BlockSpec((gate_rows, D),
                               lambda s, c: ((s * NCH + c + 1) * (CW // gate_rows), 0))],
        out_specs=pl.BlockSpec((None, None, D, CW), lambda s, c: (s, c, 0, 0)),
        out_shape=jax.ShapeDtypeStruct((N_SEG, NCH, D, CW), BF16),
        compiler_params=pltpu.CompilerParams(vmem_limit_bytes=VMEM_LIMIT),
        name="wprep",
    )(wt, wt)


def _rms_mod(x, g, shift, scale):
    y = x * lax.rsqrt(jnp.mean(x * x, axis=-1, keepdims=True) + EPS)
    return (y * g) * (1.0 + scale) + shift


def _conv_silu(z, halo_ref, c, w, b):
    tm = z.shape[0]
    zz = jnp.concatenate([halo_ref[c], z], axis=0)
    acc = b + w[CONV_K - 1:CONV_K, :] * z
    for j in range(CONV_K - 1):
        shifted = pltpu.roll(zz, CONV_K - 1 - j, axis=0)[SUBLANES:, :]
        acc = acc + w[j:j + 1, :] * shifted
    halo_ref[c] = z[tm - SUBLANES:, :]
    return acc * jax.nn.sigmoid(acc)


def _inproj_kernel(x_ref, sh_ref, sc_ref, g_ref, w_ref, wg_ref, gb_ref, cw_ref, cb_ref,
                   lng_ref, lnb_ref,
                   u_ref, v_ref, q_ref, kt_ref, mv_ref, o_ref, ga_ref, gbo_ref, gate_ref,
                   xn_scr, gv_scr, s1_scr, halo_scr):
    i = pl.program_id(0)

    @pl.when((i % (SEQ // TM_IN)) == 0)
    def _():
        halo_scr[...] = jnp.zeros(halo_scr.shape, F32)

    xn_scr[...] = _rms_mod(x_ref[...], g_ref[...], sh_ref[...], sc_ref[...]).astype(BF16)
    s1_scr[...] = jnp.zeros(s1_scr.shape, F32)

    def seg(k, c):
        return jnp.dot(xn_scr[...], w_ref[k, c], preferred_element_type=F32)

    for c in range(NCH):
        k = _conv_silu(seg(SEG_K, c), halo_scr.at[1], c, cw_ref[1, c], cb_ref[1, c])
        kt_ref[c] = k.T.astype(BF16)
        mv_ref[c] = seg(SEG_MV, c).astype(BF16)
        q = _conv_silu(seg(SEG_Q, c), halo_scr.at[0], c, cw_ref[0, c], cb_ref[0, c])
        q_ref[c] = (q * (HEAD_DIM ** -0.5)).astype(BF16)
        o_ref[c] = jax.nn.sigmoid(seg(SEG_O, c)).astype(BF16)
        gv = jax.nn.gelu(seg(SEG_V, c))
        gv_scr[c] = gv
        s1_scr[...] += gv[:, :LANES] + gv[:, LANES:]
        ga_ref[c] = jax.nn.sigmoid(seg(SEG_GA, c)).astype(BF16)
        u_ref[c] = jax.nn.gelu(seg(SEG_U, c)).astype(BF16)
        gbo_ref[c] = jax.nn.sigmoid(seg(SEG_GB, c)).astype(BF16)

    wg = jnp.concatenate([wg_ref[...], jnp.zeros((LANES - 2 * HEADS, D), F32)], axis=0)
    gp = lax.dot_general(xn_scr[...], wg.astype(BF16), (((1,), (1,)), ((), ())),
                         preferred_element_type=F32) + gb_ref[...]
    lane = lax.broadcasted_iota(jnp.int32, gp.shape, 1)
    gate_ref[...] = jnp.where(lane < HEADS, gp, jax.nn.log_sigmoid(gp))

    mu = jnp.sum(s1_scr[...], axis=-1, keepdims=True) * (1.0 / D)
    ssq = jnp.zeros((TM_IN, LANES), F32)
    for c in range(NCH):
        dv = gv_scr[c] - mu
        dv = dv * dv
        ssq = ssq + (dv[:, :LANES] + dv[:, LANES:])
    rstd = lax.rsqrt(jnp.sum(ssq, axis=-1, keepdims=True) * (1.0 / D) + EPS)
    for c in range(NCH):
        v_ref[c] = (((gv_scr[c] - mu) * rstd) * lng_ref[c] + lnb_ref[c]).astype(BF16)


def _inproj(x2, sh1, sc1, norm_g, w_all, w_gate, gate_bias, conv_w, conv_b, ln_g, ln_b):
    nsb = SEQ // TM_IN
    row = lambda i: (i, 0)
    per_batch = lambda i: (i // nsb, 0, 0)
    seg_out = pl.BlockSpec((NCH, TM_IN, CW), lambda i: (0, i, 0))
    seg_shape = jax.ShapeDtypeStruct((NCH, TOKENS, CW), BF16)
    seg_outs = [seg_out] * N_SEG
    seg_shapes = [seg_shape] * N_SEG
    seg_outs[SEG_K] = pl.BlockSpec((NCH, CW, TM_IN), lambda i: (0, 0, i))
    seg_shapes[SEG_K] = jax.ShapeDtypeStruct((NCH, CW, TOKENS), BF16)
    return pl.pallas_call(
        _inproj_kernel,
        grid=(TOKENS // TM_IN,),
        in_specs=[pl.BlockSpec((TM_IN, D), row),
                  pl.BlockSpec((None, 1, D), per_batch),
                  pl.BlockSpec((None, 1, D), per_batch),
                  _resident((1, D)),
                  _resident((N_SEG, NCH, D, CW)),
                  _resident((2 * HEADS, D)),
                  _resident((1, LANES)),
                  _resident((2, NCH, CONV_K, CW)),
                  _resident((2, NCH, 1, CW)),
                  _resident((NCH, 1, CW)),
                  _resident((NCH, 1, CW))],
        out_specs=seg_outs + [pl.BlockSpec((TM_IN, LANES), row)],
        out_shape=seg_shapes + [jax.ShapeDtypeStruct((TOKENS, LANES), F32)],
        scratch_shapes=[pltpu.VMEM((TM_IN, D), BF16),
                        pltpu.VMEM((NCH, TM_IN, CW), F32),
                        pltpu.VMEM((TM_IN, LANES), F32),
                        pltpu.VMEM((2, NCH, SUBLANES, CW), F32)],
        compiler_params=pltpu.CompilerParams(dimension_semantics=("arbitrary",),
                                             vmem_limit_bytes=VMEM_LIMIT),
        name="inproj",
    )(x2, sh1, sc1, norm_g, w_all, w_gate, gate_bias, conv_w, conv_b, ln_g, ln_b)


def _cumsum_rows(x):
    n = x.shape[0]
    row = lax.broadcasted_iota(jnp.int32, x.shape, 0)
    k = 1
    while k < n:
        x = x + jnp.where(row >= k, pltpu.roll(x, k, axis=0), 0.0)
        k *= 2
    return x


def _rep2(a):
    return jnp.concatenate([a, a], axis=1)


def _mixer_kernel(x_ref, g1_ref, u_ref, v_ref, q_ref, kt0_ref, kt1_ref, mv_ref, o_ref, ga_ref,
                  gb_ref, gate_ref, ws_ref, bsb_ref, hng_ref, wout_ref,
                  h1_ref,
                  c_scr, n_scr, m_scr, merged_scr):
    L = L_MIX
    j = pl.program_id(1)

    @pl.when(j == 0)
    def _():
        c_scr[...] = jnp.zeros(c_scr.shape, F32)
        n_scr[...] = jnp.zeros(n_scr.shape, F32)
        m_scr[...] = jnp.zeros(m_scr.shape, F32)

    kt_refs = (kt0_ref, kt1_ref)
    scans = []
    for r in range(ROWS_MIX):
        gates = gate_ref[r]
        bcum = _cumsum_rows(gates)
        scans.append((gates.T, bcum, bcum.T))
    row = lax.broadcasted_iota(jnp.int32, (L, L), 0)
    col = lax.broadcasted_iota(jnp.int32, (L, L), 1)
    causal = col <= row
    ones_rhs = jnp.ones((L, LANES), BF16)

    brow = lax.broadcasted_iota(jnp.int32, (GMLP_BLOCK, GMLP_BLOCK), 0) // CHUNK
    bcol = lax.broadcasted_iota(jnp.int32, (GMLP_BLOCK, GMLP_BLOCK), 1) // CHUNK
    wmask = bcol <= brow
    nblk = L // GMLP_BLOCK

    for h in range(HEADS):
        hs = slice(h * HEAD_DIM, (h + 1) * HEAD_DIM)
        for r in range(ROWS_MIX):
            gates_t, bcum, bcum_t = scans[r]
            li_r = gates_t[h:h + 1, :]
            b_r = bcum_t[HEADS + h:HEADS + h + 1, :]
            b_c = jnp.broadcast_to(bcum[:, HEADS + h:HEADS + h + 1], (L, LANES))
            m_prev = m_scr[r, h, 0:1, 0:1]
            dmat = jnp.where(causal, _rep2(b_c) + (li_r - b_r), -jnp.inf)
            m_inter = b_c + m_prev
            m_t = jnp.maximum(m_inter, jnp.max(dmat, axis=-1, keepdims=True))
            wts = jnp.exp(dmat - _rep2(m_t))
            qh = q_ref[h, r]
            kt = kt_refs[r][h]
            vh = mv_ref[h, r]
            s = jnp.dot(qh, kt, preferred_element_type=F32) * wts
            sb = s.astype(BF16)
            inter = jnp.exp(m_inter - m_t)
            c_old = c_scr[r, h]
            n_old = n_scr[r, h]
            n_hi = n_old.astype(BF16)
            n_lo = (n_old - n_hi.astype(F32)).astype(BF16)
            qn2 = jnp.dot(qh, jnp.concatenate([n_hi, n_lo], axis=1), preferred_element_type=F32)
            qn = qn2[:, :LANES] + qn2[:, LANES:]
            num = (jnp.dot(sb, vh, preferred_element_type=F32)
                   + _rep2(inter) * jnp.dot(qh, c_old.astype(BF16), preferred_element_type=F32))
            den = jnp.dot(sb, ones_rhs, preferred_element_type=F32) + inter * qn
            inv = 1.0 / jnp.maximum(jnp.abs(den), jnp.exp(-m_t))
            hh = num * _rep2(inv)

            b_last = b_r[:, L - 1:L]
            g_r = (b_last - b_r) + li_r
            m_new = jnp.maximum(b_last + m_prev, jnp.max(g_r, axis=-1, keepdims=True))
            decay = jnp.exp((b_last + m_prev) - m_new)
            wkt = jnp.exp(g_r - m_new).astype(BF16) * kt
            c_scr[r, h] = decay * c_old + jnp.dot(wkt, vh, preferred_element_type=F32)
            n_scr[r, h] = decay * n_old + jnp.dot(wkt, ones_rhs, preferred_element_type=F32)
            m_scr[r, h] = jnp.broadcast_to(m_new, (SUBLANES, LANES))

            mu = jnp.mean(hh, axis=-1, keepdims=True)
            dh = hh - mu
            var = jnp.mean(dh * dh, axis=-1, keepdims=True)
            y_n = (dh * lax.rsqrt(var + EPS)) * hng_ref[:, hs]

            for half in range(CW // GROUP_DIM):
                g = h * (CW // GROUP_DIM) + half
                ls = slice(half * GROUP_DIM, (half + 1) * GROUP_DIM)
                wsg = jnp.where(wmask, ws_ref[g], 0.0).astype(BF16)
                vcat = jnp.concatenate(
                    [v_ref[h, r, n * GMLP_BLOCK:(n + 1) * GMLP_BLOCK, ls] for n in range(nblk)],
                    axis=1)
                mixed = jnp.dot(wsg, vcat, preferred_element_type=F32) + _rep2(bsb_ref[g])
                for n in range(nblk):
                    rs = slice(n * GMLP_BLOCK, (n + 1) * GMLP_BLOCK)
                    gate_a = (ga_ref[h, r, rs, ls] * u_ref[h, r, rs, ls]).astype(F32)
                    gate_b = (gb_ref[h, r, rs, ls] * o_ref[h, r, rs, ls]).astype(F32)
                    merged = (gate_a * mixed[:, n * GMLP_BLOCK:(n + 1) * GMLP_BLOCK]
                              + gate_b * y_n[rs, ls])
                    merged_scr[r * L + n * GMLP_BLOCK:r * L + (n + 1) * GMLP_BLOCK,
                               g * GROUP_DIM:(g + 1) * GROUP_DIM] = merged.astype(BF16)

    proj = jnp.dot(merged_scr[...], wout_ref[...], preferred_element_type=F32)
    for r in range(ROWS_MIX):
        h1_ref[r] = x_ref[r] + g1_ref[r] * proj[r * L:(r + 1) * L, :]


def _mixer(x3, g1, u, v, q, kt, mv, o, ga, gb, gates, ws, bs_rep, hn_g, w_out):
    nsb = SEQ // L_MIX
    rows3 = pl.BlockSpec((ROWS_MIX, L_MIX, D), lambda b, j: (b, j, 0))
    seg_in = pl.BlockSpec((NCH, ROWS_MIX, L_MIX, CW), lambda b, j: (0, b, j, 0))
    kt_in = [pl.BlockSpec((NCH, CW, L_MIX), lambda b, j, r=r: (0, 0, (ROWS_MIX * b + r) * nsb + j))
             for r in range(ROWS_MIX)]
    seg4 = lambda a: a.reshape(NCH, BATCH, SEQ, CW)
    return pl.pallas_call(
        _mixer_kernel,
        grid=(BATCH // ROWS_MIX, nsb),
        in_specs=[rows3,
                  pl.BlockSpec((ROWS_MIX, 1, D), lambda b, j: (b, 0, 0)),
                  seg_in, seg_in, seg_in, kt_in[0], kt_in[1], seg_in, seg_in, seg_in, seg_in,
                  pl.BlockSpec((ROWS_MIX, L_MIX, LANES), lambda b, j: (b, j, 0)),
                  _resident((GMLP_GROUPS, GMLP_BLOCK, GMLP_BLOCK)),
                  _resident((GMLP_GROUPS, GMLP_BLOCK, LANES)),
                  _resident((1, D)),
                  _resident((D, D))],
        out_specs=rows3,
        out_shape=jax.ShapeDtypeStruct((BATCH, SEQ, D), F32),
        scratch_shapes=[pltpu.VMEM((ROWS_MIX, HEADS, HEAD_DIM, HEAD_DIM), F32),
                        pltpu.VMEM((ROWS_MIX, HEADS, HEAD_DIM, LANES), F32),
                        pltpu.VMEM((ROWS_MIX, HEADS, SUBLANES, LANES), F32),
                        pltpu.VMEM((ROWS_MIX * L_MIX, D), BF16)],
        compiler_params=pltpu.CompilerParams(dimension_semantics=("arbitrary", "arbitrary"),
                                             vmem_limit_bytes=VMEM_LIMIT),
        name="mixer",
    )(x3, g1, seg4(u), seg4(v), seg4(q), kt, kt, seg4(mv), seg4(o), seg4(ga), seg4(gb),
      gates.reshape(BATCH, SEQ, LANES), ws, bs_rep, hn_g, w_out)


def _ffn_kernel(h_ref, sh_ref, sc_ref, g2_ref, ng_ref, w1_ref, w2_ref, fg_ref, o_ref):
    h = h_ref[...]
    xn = _rms_mod(h, ng_ref[...], sh_ref[...], sc_ref[...]).astype(BF16)
    acc = jnp.zeros((TM_FFN, D), F32)
    for c in range(FFN // FFN_CHUNK):
        cs = slice(c * FFN_CHUNK, (c + 1) * FFN_CHUNK)
        a = jnp.maximum(jnp.dot(xn, w1_ref[:, cs], preferred_element_type=F32), 0.0)
        acc = acc + jnp.dot((a * a).astype(BF16), w2_ref[cs, :], preferred_element_type=F32)
    h2 = h + g2_ref[...] * acc
    y = h2 * lax.rsqrt(jnp.mean(h2 * h2, axis=-1, keepdims=True) + EPS)
    o_ref[...] = y * fg_ref[...]


def _ffn(h1, sh2, sc2, g2, norm_g, w1, w2, final_g):
    nsb = SEQ // TM_FFN
    row = lambda i: (i, 0)
    per_batch = lambda i: (i // nsb, 0, 0)
    return pl.pallas_call(
        _ffn_kernel,
        grid=(TOKENS // TM_FFN,),
        in_specs=[pl.BlockSpec((TM_FFN, D), row),
                  pl.BlockSpec((None, 1, D), per_batch),
                  pl.BlockSpec((None, 1, D), per_batch),
                  pl.BlockSpec((None, 1, D), per_batch),
                  _resident((1, D)),
                  _resident((D, FFN)),
                  _resident((FFN, D)),
                  _resident((1, D))],
        out_specs=pl.BlockSpec((TM_FFN, D), row),
        out_shape=jax.ShapeDtypeStruct((TOKENS, D), F32),
        compiler_params=pltpu.CompilerParams(dimension_semantics=("arbitrary",),
                                             vmem_limit_bytes=VMEM_LIMIT),
        name="ffn",
    )(h1, sh2, sc2, g2, norm_g, w1, w2, final_g)


def kernel(x, c, w_ada, b_ada, norm1_g, w_in, conv_w, conv_b, mlstm_gate_b, gmlp_ln_g, gmlp_ln_b,
           gmlp_ws, gmlp_bs, mlstm_hn_g, w_out, norm2_g, w_ff1, w_ff2, final_g):
    l = 0
    mod = _ada(c, w_ada[l], b_ada[l]).reshape(BATCH, 6, 1, D)
    sh1, sc1, g1, sh2, sc2, g2 = (mod[:, t] for t in range(6))

    wt = jnp.swapaxes(w_in, 1, 2)[l]
    w_all = _wprep(wt)
    w_gate = wt[OFF_I:OFF_GA]
    gate_bias = jnp.pad(mlstm_gate_b[l].reshape(1, 2 * HEADS), ((0, 0), (0, LANES - 2 * HEADS)))
    cw = conv_w[l].reshape(CONV_K, 2, NCH, CW).transpose(1, 2, 0, 3)
    cb = conv_b[l].reshape(2, NCH, 1, CW)

    x2 = x.reshape(TOKENS, D)
    u, v, q, kt, mv, o, ga, gb, gates = _inproj(
        x2, sh1, sc1, norm1_g[l].reshape(1, D), w_all, w_gate, gate_bias, cw, cb,
        gmlp_ln_g[l].reshape(NCH, 1, CW), gmlp_ln_b[l].reshape(NCH, 1, CW))

    bs_rep = jnp.broadcast_to(gmlp_bs[l][:, :, None], (GMLP_GROUPS, GMLP_BLOCK, LANES))
    h1 = _mixer(x, g1, u, v, q, kt, mv, o, ga, gb, gates,
                gmlp_ws[l], bs_rep, mlstm_hn_g[l].reshape(1, D), w_out[l].astype(BF16))
    h1 = h1.reshape(TOKENS, D)

    out = _ffn(h1, sh2, sc2, g2, norm2_g[l].reshape(1, D),
               w_ff1[l].astype(BF16), w_ff2[l].astype(BF16), final_g.reshape(1, D))
    return out.reshape(BATCH, SEQ, D)
```

```python
import jax
import jax.numpy as jnp
from jax import lax
from jax.experimental import pallas as pl
from jax.experimental.pallas import tpu as pltpu

D = 1024
BATCH = 16
SEQ = 2048
TOKENS = BATCH * SEQ
CHUNK = 64
GMLP_BLOCK = 128
GMLP_GROUPS = 8
GROUP_DIM = D // GMLP_GROUPS
HEADS = 4
HEAD_DIM = 256
CONV_K = 4
FFN = 4 * D
EPS = 1e-6
LANES = 128
SUBLANES = 8

OFF_I = 6 * D
OFF_GA = OFF_I + 2 * HEADS
N_SEG = 8
SEG_U, SEG_V, SEG_Q, SEG_K, SEG_MV, SEG_O, SEG_GA, SEG_GB = range(N_SEG)
CW = 256
NCH = D // CW

TM_IN = 512
L_MIX = 256
ROWS_MIX = 2
TM_FFN = 512
FFN_CHUNK = 1024
ADA_COLS = 1536

F32 = jnp.float32
BF16 = jnp.bfloat16
VMEM_LIMIT = 56 * 1024 * 1024


def _resident(shape):
    nd = len(shape)
    return pl.BlockSpec(shape, lambda *_: (0,) * nd, pipeline_mode=pl.Buffered(1))


def _ada_kernel(c_ref, w_ref, b_ref, o_ref):
    c = c_ref[...]
    ca = c * jax.nn.sigmoid(c)
    o_ref[...] = jnp.dot(ca, w_ref[...], preferred_element_type=F32,
                         precision=lax.Precision.HIGHEST) + b_ref[...]


def _ada(c, w, b):
    n = w.shape[1]
    return pl.pallas_call(
        _ada_kernel,
        grid=(n // ADA_COLS,),
        in_specs=[pl.BlockSpec((BATCH, D), lambda j: (0, 0)),
                  pl.BlockSpec((D, ADA_COLS), lambda j: (0, j)),
                  pl.BlockSpec((1, ADA_COLS), lambda j: (0, j))],
        out_specs=pl.BlockSpec((BATCH, ADA_COLS), lambda j: (0, j)),
        out_shape=jax.ShapeDtypeStruct((BATCH, n), F32),
        compiler_params=pltpu.CompilerParams(vmem_limit_bytes=VMEM_LIMIT),
        name="ada",
    )(c, w, b.reshape(1, n))


def _wprep_kernel(a_ref, b_ref, o_ref):
    s = pl.program_id(0)

    def emit(rows):
        for c in range(NCH):
            o_ref[c] = rows(c * CW, (c + 1) * CW).T.astype(BF16)

    @pl.when(s < SEG_GA)
    def _():
        emit(lambda lo, hi: a_ref[lo:hi, :])

    @pl.when(s >= SEG_GA)
    def _():
        skip = 2 * HEADS
        emit(lambda lo, hi: a_ref[lo + skip:hi + skip, :] if hi < D else
             jnp.concatenate([a_ref[lo + skip:, :], b_ref[...]], axis=0))


def _wprep(wt):
    gate_rows = 2 * HEADS
    return pl.pallas_call(
        _wprep_kernel,
        grid=(N_SEG,),
        in_specs=[pl.BlockSpec((D, D), lambda s: (s, 0)),
                  pl.BlockSpec((gate_rows, D), lambda s: ((s + 1) * (D // gate_rows), 0))],
        out_specs=pl.BlockSpec((None, NCH, D, CW), lambda s: (s, 0, 0, 0)),
        out_shape=jax.ShapeDtypeStruct((N_SEG, NCH, D, CW), BF16),
        compiler_params=pltpu.CompilerParams(vmem_limit_bytes=VMEM_LIMIT),
        name="wprep",
    )(wt, wt)


def _rms_mod(x, g, shift, scale):
    y = x * lax.rsqrt(jnp.mean(x * x, axis=-1, keepdims=True) + EPS)
    return (y * g) * (1.0 + scale) + shift


def _conv_silu(z, halo_ref, c, w, b):
    tm = z.shape[0]
    zz = jnp.concatenate([halo_ref[c], z], axis=0)
    acc = b + w[CONV_K - 1:CONV_K, :] * z
    for j in reversed(range(CONV_K - 1)):
        zz = pltpu.roll(zz, 1, axis=0)
        acc = acc + w[j:j + 1, :] * zz[SUBLANES:, :]
    halo_ref[c] = z[tm - SUBLANES:, :]
    return acc * jax.nn.sigmoid(acc)


def _inproj_kernel(x_ref, sh_ref, sc_ref, g_ref, w_ref, wg_ref, gb_ref, cw_ref, cb_ref,
                   lng_ref, lnb_ref,
                   u_ref, v_ref, q_ref, kt_ref, mv_ref, o_ref, ga_ref, gbo_ref, gate_ref,
                   xn_scr, gv_scr, s1_scr, halo_scr):
    i = pl.program_id(0)

    @pl.when((i % (SEQ // TM_IN)) == 0)
    def _():
        halo_scr[...] = jnp.zeros(halo_scr.shape, F32)

    xn_scr[...] = _rms_mod(x_ref[...], g_ref[...], sh_ref[...], sc_ref[...]).astype(BF16)
    s1_scr[...] = jnp.zeros(s1_scr.shape, F32)

    def seg(k, c):
        return jnp.dot(xn_scr[...], w_ref[k, c], preferred_element_type=F32)

    for c in range(NCH):
        k = _conv_silu(seg(SEG_K, c), halo_scr.at[1], c, cw_ref[1, c], cb_ref[1, c])
        kt_ref[c] = k.T.astype(BF16)
        mv_ref[c] = seg(SEG_MV, c).astype(BF16)
        q = _conv_silu(seg(SEG_Q, c), halo_scr.at[0], c, cw_ref[0, c], cb_ref[0, c])
        q_ref[c] = (q * (HEAD_DIM ** -0.5)).astype(BF16)
        o_ref[c] = jax.nn.sigmoid(seg(SEG_O, c)).astype(BF16)
        gv = jax.nn.gelu(seg(SEG_V, c))
        gv_scr[c] = gv
        s1_scr[...] += gv[:, :LANES] + gv[:, LANES:]
        ga_ref[c] = jax.nn.sigmoid(seg(SEG_GA, c)).astype(BF16)
        u_ref[c] = jax.nn.gelu(seg(SEG_U, c)).astype(BF16)
        gbo_ref[c] = jax.nn.sigmoid(seg(SEG_GB, c)).astype(BF16)

    wg = jnp.concatenate([wg_ref[...], jnp.zeros((LANES - 2 * HEADS, D), F32)], axis=0)
    gp = lax.dot_general(xn_scr[...], wg.astype(BF16), (((1,), (1,)), ((), ())),
                         preferred_element_type=F32) + gb_ref[...]
    lane = lax.broadcasted_iota(jnp.int32, gp.shape, 1)
    gate_ref[...] = jnp.where(lane < HEADS, gp, jax.nn.log_sigmoid(gp))

    mu = jnp.sum(s1_scr[...], axis=-1, keepdims=True) * (1.0 / D)
    ssq = jnp.zeros((TM_IN, LANES), F32)
    for c in range(NCH):
        dv = gv_scr[c] - mu
        dv = dv * dv
        ssq = ssq + (dv[:, :LANES] + dv[:, LANES:])
    rstd = lax.rsqrt(jnp.sum(ssq, axis=-1, keepdims=True) * (1.0 / D) + EPS)
    for c in range(NCH):
        v_ref[c] = (((gv_scr[c] - mu) * rstd) * lng_ref[c] + lnb_ref[c]).astype(BF16)


def _inproj(x2, sh1, sc1, norm_g, w_all, w_gate, gate_bias, conv_w, conv_b, ln_g, ln_b):
    nsb = SEQ // TM_IN
    row = lambda i: (i, 0)
    per_batch = lambda i: (i // nsb, 0, 0)
    seg_out = pl.BlockSpec((NCH, TM_IN, CW), lambda i: (0, i, 0))
    seg_shape = jax.ShapeDtypeStruct((NCH, TOKENS, CW), BF16)
    seg_outs = [seg_out] * N_SEG
    seg_shapes = [seg_shape] * N_SEG
    seg_outs[SEG_K] = pl.BlockSpec((NCH, CW, TM_IN), lambda i: (0, 0, i))
    seg_shapes[SEG_K] = jax.ShapeDtypeStruct((NCH, CW, TOKENS), BF16)
    return pl.pallas_call(
        _inproj_kernel,
        grid=(TOKENS // TM_IN,),
        in_specs=[pl.BlockSpec((TM_IN, D), row),
                  pl.BlockSpec((None, 1, D), per_batch),
                  pl.BlockSpec((None, 1, D), per_batch),
                  _resident((1, D)),
                  _resident((N_SEG, NCH, D, CW)),
                  _resident((2 * HEADS, D)),
                  _resident((1, LANES)),
                  _resident((2, NCH, CONV_K, CW)),
                  _resident((2, NCH, 1, CW)),
                  _resident((NCH, 1, CW)),
                  _resident((NCH, 1, CW))],
        out_specs=seg_outs + [pl.BlockSpec((TM_IN, LANES), row)],
        out_shape=seg_shapes + [jax.ShapeDtypeStruct((TOKENS, LANES), F32)],
        scratch_shapes=[pltpu.VMEM((TM_IN, D), BF16),
                        pltpu.VMEM((NCH, TM_IN, CW), F32),
                        pltpu.VMEM((TM_IN, LANES), F32),
                        pltpu.VMEM((2, NCH, SUBLANES, CW), F32)],
        compiler_params=pltpu.CompilerParams(dimension_semantics=("arbitrary",),
                                             vmem_limit_bytes=VMEM_LIMIT),
        name="inproj",
    )(x2, sh1, sc1, norm_g, w_all, w_gate, gate_bias, conv_w, conv_b, ln_g, ln_b)


def _cumsum_rows(x):
    n = x.shape[0]
    row = lax.broadcasted_iota(jnp.int32, x.shape, 0)
    k = 1
    while k < n:
        x = x + jnp.where(row >= k, pltpu.roll(x, k, axis=0), 0.0)
        k *= 2
    return x


def _rep2(a):
    return jnp.concatenate([a, a], axis=1)


def _mixer_kernel(x_ref, g1_ref, u_ref, v_ref, q_ref, kt0_ref, kt1_ref, mv_ref, o_ref, ga_ref,
                  gb_ref, gate_ref, ws_ref, bsb_ref, hng_ref, wout_ref,
                  h1_ref,
                  c_scr, n_scr, m_scr, merged_scr):
    L = L_MIX
    j = pl.program_id(1)

    @pl.when(j == 0)
    def _():
        c_scr[...] = jnp.zeros(c_scr.shape, F32)
        n_scr[...] = jnp.zeros(n_scr.shape, F32)
        m_scr[...] = jnp.zeros(m_scr.shape, F32)

    kt_refs = (kt0_ref, kt1_ref)
    scans = []
    for r in range(ROWS_MIX):
        gates = gate_ref[r]
        bcum = _cumsum_rows(gates)
        scans.append((gates.T, bcum, bcum.T))
    row = lax.broadcasted_iota(jnp.int32, (L, L), 0)
    col = lax.broadcasted_iota(jnp.int32, (L, L), 1)
    causal = col <= row
    ones_rhs = jnp.ones((L, LANES), BF16)

    brow = lax.broadcasted_iota(jnp.int32, (GMLP_BLOCK, GMLP_BLOCK), 0) // CHUNK
    bcol = lax.broadcasted_iota(jnp.int32, (GMLP_BLOCK, GMLP_BLOCK), 1) // CHUNK
    wmask = bcol <= brow
    nblk = L // GMLP_BLOCK

    for h in range(HEADS):
        hs = slice(h * HEAD_DIM, (h + 1) * HEAD_DIM)
        for r in range(ROWS_MIX):
            gates_t, bcum, bcum_t = scans[r]
            li_r = gates_t[h:h + 1, :]
            b_r = bcum_t[HEADS + h:HEADS + h + 1, :]
            b_c = jnp.broadcast_to(bcum[:, HEADS + h:HEADS + h + 1], (L, LANES))
            m_prev = m_scr[r, h, 0:1, 0:1]
            dmat = jnp.where(causal, _rep2(b_c) + (li_r - b_r), -jnp.inf)
            m_inter = b_c + m_prev
            m_t = jnp.maximum(m_inter, jnp.max(dmat, axis=-1, keepdims=True))
            wts = jnp.exp(dmat - _rep2(m_t))
            qh = q_ref[h, r]
            kt = kt_refs[r][h]
            vh = mv_ref[h, r]
            s = jnp.dot(qh, kt, preferred_element_type=F32) * wts
            sb = s.astype(BF16)
            inter = jnp.exp(m_inter - m_t)
            c_old = c_scr[r, h]
            n_old = n_scr[r, h]
            n_hi = n_old.astype(BF16)
            n_lo = (n_old - n_hi.astype(F32)).astype(BF16)
            qn2 = jnp.dot(qh, jnp.concatenate([n_hi, n_lo], axis=1), preferred_element_type=F32)
            qn = qn2[:, :LANES] + qn2[:, LANES:]
            num = (jnp.dot(sb, vh, preferred_element_type=F32)
                   + _rep2(inter) * jnp.dot(qh, c_old.astype(BF16), preferred_element_type=F32))
            den = jnp.dot(sb, ones_rhs, preferred_element_type=F32) + inter * qn
            inv = 1.0 / jnp.maximum(jnp.abs(den), jnp.exp(-m_t))
            hh = num * _rep2(inv)

            b_last = b_r[:, L - 1:L]
            g_r = (b_last - b_r) + li_r
            m_new = jnp.maximum(b_last + m_prev, jnp.max(g_r, axis=-1, keepdims=True))
            decay = jnp.exp((b_last + m_prev) - m_new)
            wkt = jnp.exp(g_r - m_new).astype(BF16) * kt
            c_scr[r, h] = decay * c_old + jnp.dot(wkt, vh, preferred_element_type=F32)
            n_scr[r, h] = decay * n_old + jnp.dot(wkt, ones_rhs, preferred_element_type=F32)
            m_scr[r, h] = jnp.broadcast_to(m_new, (SUBLANES, LANES))

            mu = jnp.mean(hh, axis=-1, keepdims=True)
            dh = hh - mu
            var = jnp.mean(dh * dh, axis=-1, keepdims=True)
            y_n = (dh * lax.rsqrt(var + EPS)) * hng_ref[:, hs]

            for half in range(CW // GROUP_DIM):
                g = h * (CW // GROUP_DIM) + half
                ls = slice(half * GROUP_DIM, (half + 1) * GROUP_DIM)
                wsg = jnp.where(wmask, ws_ref[g], 0.0).astype(BF16)
                vcat = jnp.concatenate(
                    [v_ref[h, r, n * GMLP_BLOCK:(n + 1) * GMLP_BLOCK, ls] for n in range(nblk)],
                    axis=1)
                mixed = jnp.dot(wsg, vcat, preferred_element_type=F32) + _rep2(bsb_ref[g])
                for n in range(nblk):
                    rs = slice(n * GMLP_BLOCK, (n + 1) * GMLP_BLOCK)
                    gate_a = (ga_ref[h, r, rs, ls] * u_ref[h, r, rs, ls]).astype(F32)
                    gate_b = (gb_ref[h, r, rs, ls] * o_ref[h, r, rs, ls]).astype(F32)
                    merged = (gate_a * mixed[:, n * GMLP_BLOCK:(n + 1) * GMLP_BLOCK]
                              + gate_b * y_n[rs, ls])
                    merged_scr[r * L + n * GMLP_BLOCK:r * L + (n + 1) * GMLP_BLOCK,
                               g * GROUP_DIM:(g + 1) * GROUP_DIM] = merged.astype(BF16)

    proj = jnp.dot(merged_scr[...], wout_ref[...], preferred_element_type=F32)
    for r in range(ROWS_MIX):
        h1_ref[r] = x_ref[r] + g1_ref[r] * proj[r * L:(r + 1) * L, :]


def _mixer(x3, g1, u, v, q, kt, mv, o, ga, gb, gates, ws, bs_rep, hn_g, w_out):
    nsb = SEQ // L_MIX
    rows3 = pl.BlockSpec((ROWS_MIX, L_MIX, D), lambda b, j: (b, j, 0))
    seg_in = pl.BlockSpec((NCH, ROWS_MIX, L_MIX, CW), lambda b, j: (0, b, j, 0))
    kt_in = [pl.BlockSpec((NCH, CW, L_MIX), lambda b, j, r=r: (0, 0, (ROWS_MIX * b + r) * nsb + j))
             for r in range(ROWS_MIX)]
    seg4 = lambda a: a.reshape(NCH, BATCH, SEQ, CW)
    return pl.pallas_call(
        _mixer_kernel,
        grid=(BATCH // ROWS_MIX, nsb),
        in_specs=[rows3,
                  pl.BlockSpec((ROWS_MIX, 1, D), lambda b, j: (b, 0, 0)),
                  seg_in, seg_in, seg_in, kt_in[0], kt_in[1], seg_in, seg_in, seg_in, seg_in,
                  pl.BlockSpec((ROWS_MIX, L_MIX, LANES), lambda b, j: (b, j, 0)),
                  _resident((GMLP_GROUPS, GMLP_BLOCK, GMLP_BLOCK)),
                  _resident((GMLP_GROUPS, GMLP_BLOCK, LANES)),
                  _resident((1, D)),
                  _resident((D, D))],
        out_specs=rows3,
        out_shape=jax.ShapeDtypeStruct((BATCH, SEQ, D), F32),
        scratch_shapes=[pltpu.VMEM((ROWS_MIX, HEADS, HEAD_DIM, HEAD_DIM), F32),
                        pltpu.VMEM((ROWS_MIX, HEADS, HEAD_DIM, LANES), F32),
                        pltpu.VMEM((ROWS_MIX, HEADS, SUBLANES, LANES), F32),
                        pltpu.VMEM((ROWS_MIX * L_MIX, D), BF16)],
        compiler_params=pltpu.CompilerParams(dimension_semantics=("arbitrary", "arbitrary"),
                                             vmem_limit_bytes=VMEM_LIMIT),
        name="mixer",
    )(x3, g1, seg4(u), seg4(v), seg4(q), kt, kt, seg4(mv), seg4(o), seg4(ga), seg4(gb),
      gates.reshape(BATCH, SEQ, LANES), ws, bs_rep, hn_g, w_out)


def _ffn_kernel(h_ref, sh_ref, sc_ref, g2_ref, ng_ref, w1_ref, w2_ref, fg_ref, o_ref):
    h = h_ref[...]
    xn = _rms_mod(h, ng_ref[...], sh_ref[...], sc_ref[...]).astype(BF16)
    acc = jnp.zeros((TM_FFN, D), F32)
    for c in range(FFN // FFN_CHUNK):
        cs = slice(c * FFN_CHUNK, (c + 1) * FFN_CHUNK)
        a = jnp.maximum(jnp.dot(xn, w1_ref[:, cs], preferred_element_type=F32), 0.0)
        acc = acc + jnp.dot((a * a).astype(BF16), w2_ref[cs, :], preferred_element_type=F32)
    h2 = h + g2_ref[...] * acc
    y = h2 * lax.rsqrt(jnp.mean(h2 * h2, axis=-1, keepdims=True) + EPS)
    o_ref[...] = y * fg_ref[...]


def _ffn(h1, sh2, sc2, g2, norm_g, w1, w2, final_g):
    nsb = SEQ // TM_FFN
    row = lambda i: (i, 0)
    per_batch = lambda i: (i // nsb, 0, 0)
    return pl.pallas_call(
        _ffn_kernel,
        grid=(TOKENS // TM_FFN,),
        in_specs=[pl.BlockSpec((TM_FFN, D), row),
                  pl.BlockSpec((None, 1, D), per_batch),
                  pl.BlockSpec((None, 1, D), per_batch),
                  pl.BlockSpec((None, 1, D), per_batch),
                  _resident((1, D)),
                  _resident((D, FFN)),
                  _resident((FFN, D)),
                  _resident((1, D))],
        out_specs=pl.BlockSpec((TM_FFN, D), row),
        out_shape=jax.ShapeDtypeStruct((TOKENS, D), F32),
        compiler_params=pltpu.CompilerParams(dimension_semantics=("arbitrary",),
                                             vmem_limit_bytes=VMEM_LIMIT),
        name="ffn",
    )(h1, sh2, sc2, g2, norm_g, w1, w2, final_g)


def kernel(x, c, w_ada, b_ada, norm1_g, w_in, conv_w, conv_b, mlstm_gate_b, gmlp_ln_g, gmlp_ln_b,
           gmlp_ws, gmlp_bs, mlstm_hn_g, w_out, norm2_g, w_ff1, w_ff2, final_g):
    l = 0
    mod = _ada(c, w_ada[l], b_ada[l]).reshape(BATCH, 6, 1, D)
    sh1, sc1, g1, sh2, sc2, g2 = (mod[:, t] for t in range(6))

    wt = jnp.swapaxes(w_in, 1, 2)[l]
    w_all = _wprep(wt)
    w_gate = wt[OFF_I:OFF_GA]
    gate_bias = jnp.pad(mlstm_gate_b[l].reshape(1, 2 * HEADS), ((0, 0), (0, LANES - 2 * HEADS)))
    cw = conv_w[l].reshape(CONV_K, 2, NCH, CW).transpose(1, 2, 0, 3)
    cb = conv_b[l].reshape(2, NCH, 1, CW)

    x2 = x.reshape(TOKENS, D)
    u, v, q, kt, mv, o, ga, gb, gates = _inproj(
        x2, sh1, sc1, norm1_g[l].reshape(1, D), w_all, w_gate, gate_bias, cw, cb,
        gmlp_ln_g[l].reshape(NCH, 1, CW), gmlp_ln_b[l].reshape(NCH, 1, CW))

    bs_rep = jnp.broadcast_to(gmlp_bs[l][:, :, None], (GMLP_GROUPS, GMLP_BLOCK, LANES))
    h1 = _mixer(x, g1, u, v, q, kt, mv, o, ga, gb, gates,
                gmlp_ws[l], bs_rep, mlstm_hn_g[l].reshape(1, D), w_out[l].astype(BF16))
    h1 = h1.reshape(TOKENS, D)

    out = _ffn(h1, sh2, sc2, g2, norm2_g[l].reshape(1, D),
               w_ff1[l].astype(BF16), w_ff2[l].astype(BF16), final_g.reshape(1, D))
    return out.reshape(BATCH, SEQ, D)
```

```python
import jax
import jax.numpy as jnp
from jax import lax
from jax.experimental import pallas as pl
from jax.experimental.pallas import tpu as pltpu

D = 1024
BATCH = 16
SEQ = 2048
TOKENS = BATCH * SEQ
CHUNK = 64
GMLP_BLOCK = 128
GMLP_GROUPS = 8
GROUP_DIM = D // GMLP_GROUPS
HEADS = 4
HEAD_DIM = 256
CONV_K = 4
FFN = 4 * D
EPS = 1e-6
LANES = 128
SUBLANES = 8

OFF_I = 6 * D
OFF_GA = OFF_I + 2 * HEADS
N_SEG = 8
SEG_U, SEG_V, SEG_Q, SEG_K, SEG_MV, SEG_O, SEG_GA, SEG_GB = range(N_SEG)
CW = 256
NCH = D // CW

TM_IN = 512
L_MIX = 256
ROWS_MIX = 2
TM_FFN = 512
FFN_CHUNK = 1024
ADA_COLS = 1536

F32 = jnp.float32
BF16 = jnp.bfloat16
VMEM_LIMIT = 56 * 1024 * 1024


def _resident(shape):
    nd = len(shape)
    return pl.BlockSpec(shape, lambda *_: (0,) * nd, pipeline_mode=pl.Buffered(1))


def _ada_kernel(c_ref, w_ref, b_ref, o_ref):
    c = c_ref[...]
    ca = c * jax.nn.sigmoid(c)
    o_ref[...] = jnp.dot(ca, w_ref[...], preferred_element_type=F32,
                         precision=lax.Precision.HIGHEST) + b_ref[...]


def _ada(c, w, b):
    n = w.shape[1]
    return pl.pallas_call(
        _ada_kernel,
        grid=(n // ADA_COLS,),
        in_specs=[pl.BlockSpec((BATCH, D), lambda j: (0, 0)),
                  pl.BlockSpec((D, ADA_COLS), lambda j: (0, j)),
                  pl.BlockSpec((1, ADA_COLS), lambda j: (0, j))],
        out_specs=pl.BlockSpec((BATCH, ADA_COLS), lambda j: (0, j)),
        out_shape=jax.ShapeDtypeStruct((BATCH, n), F32),
        compiler_params=pltpu.CompilerParams(vmem_limit_bytes=VMEM_LIMIT),
        name="ada",
    )(c, w, b.reshape(1, n))


def _wprep_kernel(a_ref, b_ref, o_ref):
    s = pl.program_id(0)

    def emit(rows):
        for c in range(NCH):
            o_ref[c] = rows(c * CW, (c + 1) * CW).T.astype(BF16)

    @pl.when(s < SEG_GA)
    def _():
        emit(lambda lo, hi: a_ref[lo:hi, :])

    @pl.when(s >= SEG_GA)
    def _():
        skip = 2 * HEADS
        emit(lambda lo, hi: a_ref[lo + skip:hi + skip, :] if hi < D else
             jnp.concatenate([a_ref[lo + skip:, :], b_ref[...]], axis=0))


def _wprep(wt):
    gate_rows = 2 * HEADS
    return pl.pallas_call(
        _wprep_kernel,
        grid=(N_SEG,),
        in_specs=[pl.BlockSpec((D, D), lambda s: (s, 0)),
                  pl.BlockSpec((gate_rows, D), lambda s: ((s + 1) * (D // gate_rows), 0))],
        out_specs=pl.BlockSpec((None, NCH, D, CW), lambda s: (s, 0, 0, 0)),
        out_shape=jax.ShapeDtypeStruct((N_SEG, NCH, D, CW), BF16),
        compiler_params=pltpu.CompilerParams(vmem_limit_bytes=VMEM_LIMIT),
        name="wprep",
    )(wt, wt)


def _rms_mod(x, g, shift, scale):
    y = x * lax.rsqrt(jnp.mean(x * x, axis=-1, keepdims=True) + EPS)
    return (y * g) * (1.0 + scale) + shift


def _conv_silu(z, halo_ref, c, w, b):
    assert CONV_K == 4 and w.shape[0] == CONV_K
    tm = z.shape[0]
    zz = jnp.concatenate([halo_ref[c], z], axis=0)
    z1 = pltpu.roll(zz, 1, axis=0)
    older = pltpu.roll(w[1:2, :] * zz + w[0:1, :] * z1, 2, axis=0)
    acc = (b + w[3:4, :] * z + w[2:3, :] * z1[SUBLANES:, :]) + older[SUBLANES:, :]
    halo_ref[c] = z[tm - SUBLANES:, :]
    return acc * jax.nn.sigmoid(acc)


def _inproj_kernel(x_ref, sh_ref, sc_ref, g_ref, w_ref, wg_ref, gb_ref, cw_ref, cb_ref,
                   lng_ref, lnb_ref,
                   u_ref, v_ref, q_ref, kt_ref, mv_ref, o_ref, ga_ref, gbo_ref, gate_ref,
                   xn_scr, gv_scr, s1_scr, halo_scr):
    i = pl.program_id(0)

    @pl.when((i % (SEQ // TM_IN)) == 0)
    def _():
        halo_scr[...] = jnp.zeros(halo_scr.shape, F32)

    xn_scr[...] = _rms_mod(x_ref[...], g_ref[...], sh_ref[...], sc_ref[...]).astype(BF16)
    s1_scr[...] = jnp.zeros(s1_scr.shape, F32)

    def seg(k, c):
        return jnp.dot(xn_scr[...], w_ref[k, c], preferred_element_type=F32)

    for c in range(NCH):
        k = _conv_silu(seg(SEG_K, c), halo_scr.at[1], c, cw_ref[1, c], cb_ref[1, c])
        kt_ref[c] = k.T.astype(BF16)
        mv_ref[c] = seg(SEG_MV, c).astype(BF16)
        q = _conv_silu(seg(SEG_Q, c), halo_scr.at[0], c, cw_ref[0, c], cb_ref[0, c])
        q_ref[c] = (q * (HEAD_DIM ** -0.5)).astype(BF16)
        o_ref[c] = jax.nn.sigmoid(seg(SEG_O, c)).astype(BF16)
        gv = jax.nn.gelu(seg(SEG_V, c))
        gv_scr[c] = gv
        s1_scr[...] += gv[:, :LANES] + gv[:, LANES:]
        ga_ref[c] = jax.nn.sigmoid(seg(SEG_GA, c)).astype(BF16)
        u_ref[c] = jax.nn.gelu(seg(SEG_U, c)).astype(BF16)
        gbo_ref[c] = jax.nn.sigmoid(seg(SEG_GB, c)).astype(BF16)

    wg = jnp.concatenate([wg_ref[...], jnp.zeros((LANES - 2 * HEADS, D), F32)], axis=0)
    gp = lax.dot_general(xn_scr[...], wg.astype(BF16), (((1,), (1,)), ((), ())),
                         preferred_element_type=F32) + gb_ref[...]
    lane = lax.broadcasted_iota(jnp.int32, gp.shape, 1)
    gate_ref[...] = jnp.where(lane < HEADS, gp, jax.nn.log_sigmoid(gp))

    mu = jnp.sum(s1_scr[...], axis=-1, keepdims=True) * (1.0 / D)
    ssq = jnp.zeros((TM_IN, LANES), F32)
    for c in range(NCH):
        dv = gv_scr[c] - mu
        dv = dv * dv
        ssq = ssq + (dv[:, :LANES] + dv[:, LANES:])
    rstd = lax.rsqrt(jnp.sum(ssq, axis=-1, keepdims=True) * (1.0 / D) + EPS)
    for c in range(NCH):
        v_ref[c] = (((gv_scr[c] - mu) * rstd) * lng_ref[c] + lnb_ref[c]).astype(BF16)


def _inproj(x2, sh1, sc1, norm_g, w_all, w_gate, gate_bias, conv_w, conv_b, ln_g, ln_b):
    nsb = SEQ // TM_IN
    row = lambda i: (i, 0)
    per_batch = lambda i: (i // nsb, 0, 0)
    seg_out = pl.BlockSpec((NCH, TM_IN, CW), lambda i: (0, i, 0))
    seg_shape = jax.ShapeDtypeStruct((NCH, TOKENS, CW), BF16)
    seg_outs = [seg_out] * N_SEG
    seg_shapes = [seg_shape] * N_SEG
    seg_outs[SEG_K] = pl.BlockSpec((NCH, CW, TM_IN), lambda i: (0, 0, i))
    seg_shapes[SEG_K] = jax.ShapeDtypeStruct((NCH, CW, TOKENS), BF16)
    return pl.pallas_call(
        _inproj_kernel,
        grid=(TOKENS // TM_IN,),
        in_specs=[pl.BlockSpec((TM_IN, D), row),
                  pl.BlockSpec((None, 1, D), per_batch),
                  pl.BlockSpec((None, 1, D), per_batch),
                  _resident((1, D)),
                  _resident((N_SEG, NCH, D, CW)),
                  _resident((2 * HEADS, D)),
                  _resident((1, LANES)),
                  _resident((2, NCH, CONV_K, CW)),
                  _resident((2, NCH, 1, CW)),
                  _resident((NCH, 1, CW)),
                  _resident((NCH, 1, CW))],
        out_specs=seg_outs + [pl.BlockSpec((TM_IN, LANES), row)],
        out_shape=seg_shapes + [jax.ShapeDtypeStruct((TOKENS, LANES), F32)],
        scratch_shapes=[pltpu.VMEM((TM_IN, D), BF16),
                        pltpu.VMEM((NCH, TM_IN, CW), F32),
                        pltpu.VMEM((TM_IN, LANES), F32),
                        pltpu.VMEM((2, NCH, SUBLANES, CW), F32)],
        compiler_params=pltpu.CompilerParams(dimension_semantics=("arbitrary",),
                                             vmem_limit_bytes=VMEM_LIMIT),
        name="inproj",
    )(x2, sh1, sc1, norm_g, w_all, w_gate, gate_bias, conv_w, conv_b, ln_g, ln_b)


def _cumsum_rows(x):
    n = x.shape[0]
    row = lax.broadcasted_iota(jnp.int32, x.shape, 0)
    k = 1
    while k < n:
        x = x + jnp.where(row >= k, pltpu.roll(x, k, axis=0), 0.0)
        k *= 2
    return x


def _rep2(a):
    return jnp.concatenate([a, a], axis=1)


def _mixer_kernel(x_ref, g1_ref, u_ref, v_ref, q_ref, kt0_ref, kt1_ref, mv_ref, o_ref, ga_ref,
                  gb_ref, gate_ref, ws_ref, bsb_ref, hng_ref, wout_ref,
                  h1_ref,
                  c_scr, n_scr, m_scr, merged_scr):
    L = L_MIX
    j = pl.program_id(1)

    @pl.when(j == 0)
    def _():
        c_scr[...] = jnp.zeros(c_scr.shape, F32)
        n_scr[...] = jnp.zeros(n_scr.shape, F32)
        m_scr[...] = jnp.zeros(m_scr.shape, F32)

    kt_refs = (kt0_ref, kt1_ref)
    scans = []
    for r in range(ROWS_MIX):
        gates = gate_ref[r]
        bcum = _cumsum_rows(gates)
        scans.append((gates.T, bcum, bcum.T))
    row = lax.broadcasted_iota(jnp.int32, (L, L), 0)
    col = lax.broadcasted_iota(jnp.int32, (L, L), 1)
    causal = col <= row
    ones_rhs = jnp.ones((L, LANES), BF16)

    brow = lax.broadcasted_iota(jnp.int32, (GMLP_BLOCK, GMLP_BLOCK), 0) // CHUNK
    bcol = lax.broadcasted_iota(jnp.int32, (GMLP_BLOCK, GMLP_BLOCK), 1) // CHUNK
    wmask = bcol <= brow
    nblk = L // GMLP_BLOCK

    for h in range(HEADS):
        hs = slice(h * HEAD_DIM, (h + 1) * HEAD_DIM)
        for r in range(ROWS_MIX):
            gates_t, bcum, bcum_t = scans[r]
            li_r = gates_t[h:h + 1, :]
            b_r = bcum_t[HEADS + h:HEADS + h + 1, :]
            b_c = jnp.broadcast_to(bcum[:, HEADS + h:HEADS + h + 1], (L, LANES))
            m_prev = m_scr[r, h, 0:1, 0:1]
            dmat = jnp.where(causal, _rep2(b_c) + (li_r - b_r), -jnp.inf)
            m_inter = b_c + m_prev
            m_t = jnp.maximum(m_inter, jnp.max(dmat, axis=-1, keepdims=True))
            wts = jnp.exp(dmat - _rep2(m_t))
            qh = q_ref[h, r]
            kt = kt_refs[r][h]
            vh = mv_ref[h, r]
            s = jnp.dot(qh, kt, preferred_element_type=F32) * wts
            sb = s.astype(BF16)
            inter = jnp.exp(m_inter - m_t)
            c_old = c_scr[r, h]
            n_old = n_scr[r, h]
            n_hi = n_old.astype(BF16)
            n_lo = (n_old - n_hi.astype(F32)).astype(BF16)
            qn2 = jnp.dot(qh, jnp.concatenate([n_hi, n_lo], axis=1), preferred_element_type=F32)
            qn = qn2[:, :LANES] + qn2[:, LANES:]
            num = (jnp.dot(sb, vh, preferred_element_type=F32)
                   + _rep2(inter) * jnp.dot(qh, c_old.astype(BF16), preferred_element_type=F32))
            den = jnp.dot(sb, ones_rhs, preferred_element_type=F32) + inter * qn
            inv = 1.0 / jnp.maximum(jnp.abs(den), jnp.exp(-m_t))
            hh = num * _rep2(inv)

            b_last = b_r[:, L - 1:L]
            g_r = (b_last - b_r) + li_r
            m_new = jnp.maximum(b_last + m_prev, jnp.max(g_r, axis=-1, keepdims=True))
            decay = jnp.exp((b_last + m_prev) - m_new)
            wkt = jnp.exp(g_r - m_new).astype(BF16) * kt
            c_scr[r, h] = decay * c_old + jnp.dot(wkt, vh, preferred_element_type=F32)
            n_scr[r, h] = decay * n_old + jnp.dot(wkt, ones_rhs, preferred_element_type=F32)
            m_scr[r, h] = jnp.broadcast_to(m_new, (SUBLANES, LANES))

            mu = jnp.mean(hh, axis=-1, keepdims=True)
            dh = hh - mu
            var = jnp.mean(dh * dh, axis=-1, keepdims=True)
            y_n = (dh * lax.rsqrt(var + EPS)) * hng_ref[:, hs]

            for half in range(CW // GROUP_DIM):
                g = h * (CW // GROUP_DIM) + half
                ls = slice(half * GROUP_DIM, (half + 1) * GROUP_DIM)
                wsg = jnp.where(wmask, ws_ref[g], 0.0).astype(BF16)
                vcat = jnp.concatenate(
                    [v_ref[h, r, n * GMLP_BLOCK:(n + 1) * GMLP_BLOCK, ls] for n in range(nblk)],
                    axis=1)
                mixed = jnp.dot(wsg, vcat, preferred_element_type=F32) + _rep2(bsb_ref[g])
                for n in range(nblk):
                    rs = slice(n * GMLP_BLOCK, (n + 1) * GMLP_BLOCK)
                    gate_a = (ga_ref[h, r, rs, ls] * u_ref[h, r, rs, ls]).astype(F32)
                    gate_b = (gb_ref[h, r, rs, ls] * o_ref[h, r, rs, ls]).astype(F32)
                    merged = (gate_a * mixed[:, n * GMLP_BLOCK:(n + 1) * GMLP_BLOCK]
                              + gate_b * y_n[rs, ls])
                    merged_scr[r * L + n * GMLP_BLOCK:r * L + (n + 1) * GMLP_BLOCK,
                               g * GROUP_DIM:(g + 1) * GROUP_DIM] = merged.astype(BF16)

    proj = jnp.dot(merged_scr[...], wout_ref[...], preferred_element_type=F32)
    for r in range(ROWS_MIX):
        h1_ref[r] = x_ref[r] + g1_ref[r] * proj[r * L:(r + 1) * L, :]


def _mixer(x3, g1, u, v, q, kt, mv, o, ga, gb, gates, ws, bs_rep, hn_g, w_out):
    nsb = SEQ // L_MIX
    rows3 = pl.BlockSpec((ROWS_MIX, L_MIX, D), lambda b, j: (b, j, 0))
    seg_in = pl.BlockSpec((NCH, ROWS_MIX, L_MIX, CW), lambda b, j: (0, b, j, 0))
    kt_in = [pl.BlockSpec((NCH, CW, L_MIX), lambda b, j, r=r: (0, 0, (ROWS_MIX * b + r) * nsb + j))
             for r in range(ROWS_MIX)]
    seg4 = lambda a: a.reshape(NCH, BATCH, SEQ, CW)
    return pl.pallas_call(
        _mixer_kernel,
        grid=(BATCH // ROWS_MIX, nsb),
        in_specs=[rows3,
                  pl.BlockSpec((ROWS_MIX, 1, D), lambda b, j: (b, 0, 0)),
                  seg_in, seg_in, seg_in, kt_in[0], kt_in[1], seg_in, seg_in, seg_in, seg_in,
                  pl.BlockSpec((ROWS_MIX, L_MIX, LANES), lambda b, j: (b, j, 0)),
                  _resident((GMLP_GROUPS, GMLP_BLOCK, GMLP_BLOCK)),
                  _resident((GMLP_GROUPS, GMLP_BLOCK, LANES)),
                  _resident((1, D)),
                  _resident((D, D))],
        out_specs=rows3,
        out_shape=jax.ShapeDtypeStruct((BATCH, SEQ, D), F32),
        scratch_shapes=[pltpu.VMEM((ROWS_MIX, HEADS, HEAD_DIM, HEAD_DIM), F32),
                        pltpu.VMEM((ROWS_MIX, HEADS, HEAD_DIM, LANES), F32),
                        pltpu.VMEM((ROWS_MIX, HEADS, SUBLANES, LANES), F32),
                        pltpu.VMEM((ROWS_MIX * L_MIX, D), BF16)],
        compiler_params=pltpu.CompilerParams(dimension_semantics=("arbitrary", "arbitrary"),
                                             vmem_limit_bytes=VMEM_LIMIT),
        name="mixer",
    )(x3, g1, seg4(u), seg4(v), seg4(q), kt, kt, seg4(mv), seg4(o), seg4(ga), seg4(gb),
      gates.reshape(BATCH, SEQ, LANES), ws, bs_rep, hn_g, w_out)


def _ffn_kernel(h_ref, sh_ref, sc_ref, g2_ref, ng_ref, w1_ref, w2_ref, fg_ref, o_ref):
    h = h_ref[...]
    xn = _rms_mod(h, ng_ref[...], sh_ref[...], sc_ref[...]).astype(BF16)
    acc = jnp.zeros((TM_FFN, D), F32)
    for c in range(FFN // FFN_CHUNK):
        cs = slice(c * FFN_CHUNK, (c + 1) * FFN_CHUNK)
        a = jnp.maximum(jnp.dot(xn, w1_ref[:, cs], preferred_element_type=F32), 0.0)
        acc = acc + jnp.dot((a * a).astype(BF16), w2_ref[cs, :], preferred_element_type=F32)
    h2 = h + g2_ref[...] * acc
    y = h2 * lax.rsqrt(jnp.mean(h2 * h2, axis=-1, keepdims=True) + EPS)
    o_ref[...] = y * fg_ref[...]


def _ffn(h1, sh2, sc2, g2, norm_g, w1, w2, final_g):
    nsb = SEQ // TM_FFN
    row = lambda i: (i, 0)
    per_batch = lambda i: (i // nsb, 0, 0)
    return pl.pallas_call(
        _ffn_kernel,
        grid=(TOKENS // TM_FFN,),
        in_specs=[pl.BlockSpec((TM_FFN, D), row),
                  pl.BlockSpec((None, 1, D), per_batch),
                  pl.BlockSpec((None, 1, D), per_batch),
                  pl.BlockSpec((None, 1, D), per_batch),
                  _resident((1, D)),
                  _resident((D, FFN)),
                  _resident((FFN, D)),
                  _resident((1, D))],
        out_specs=pl.BlockSpec((TM_FFN, D), row),
        out_shape=jax.ShapeDtypeStruct((TOKENS, D), F32),
        compiler_params=pltpu.CompilerParams(dimension_semantics=("arbitrary",),
                                             vmem_limit_bytes=VMEM_LIMIT),
        name="ffn",
    )(h1, sh2, sc2, g2, norm_g, w1, w2, final_g)


def kernel(x, c, w_ada, b_ada, norm1_g, w_in, conv_w, conv_b, mlstm_gate_b, gmlp_ln_g, gmlp_ln_b,
           gmlp_ws, gmlp_bs, mlstm_hn_g, w_out, norm2_g, w_ff1, w_ff2, final_g):
    l = 0
    mod = _ada(c, w_ada[l], b_ada[l]).reshape(BATCH, 6, 1, D)
    sh1, sc1, g1, sh2, sc2, g2 = (mod[:, t] for t in range(6))

    wt = jnp.swapaxes(w_in, 1, 2)[l]
    w_all = _wprep(wt)
    w_gate = wt[OFF_I:OFF_GA]
    gate_bias = jnp.pad(mlstm_gate_b[l].reshape(1, 2 * HEADS), ((0, 0), (0, LANES - 2 * HEADS)))
    cw = conv_w[l].reshape(CONV_K, 2, NCH, CW).transpose(1, 2, 0, 3)
    cb = conv_b[l].reshape(2, NCH, 1, CW)

    x2 = x.reshape(TOKENS, D)
    u, v, q, kt, mv, o, ga, gb, gates = _inproj(
        x2, sh1, sc1, norm1_g[l].reshape(1, D), w_all, w_gate, gate_bias, cw, cb,
        gmlp_ln_g[l].reshape(NCH, 1, CW), gmlp_ln_b[l].reshape(NCH, 1, CW))

    bs_rep = jnp.broadcast_to(gmlp_bs[l][:, :, None], (GMLP_GROUPS, GMLP_BLOCK, LANES))
    h1 = _mixer(x, g1, u, v, q, kt, mv, o, ga, gb, gates,
                gmlp_ws[l], bs_rep, mlstm_hn_g[l].reshape(1, D), w_out[l].astype(BF16))
    h1 = h1.reshape(TOKENS, D)

    out = _ffn(h1, sh2, sc2, g2, norm2_g[l].reshape(1, D),
               w_ff1[l].astype(BF16), w_ff2[l].astype(BF16), final_g.reshape(1, D))
    return out.reshape(BATCH, SEQ, D)
```

```python
import jax
import jax.numpy as jnp
from jax import lax
from jax.experimental import pallas as pl
from jax.experimental.pallas import tpu as pltpu

D = 1024
BATCH = 16
SEQ = 2048
TOKENS = BATCH * SEQ
CHUNK = 64
GMLP_BLOCK = 128
GMLP_GROUPS = 8
GROUP_DIM = D // GMLP_GROUPS
HEADS = 4
HEAD_DIM = 256
CONV_K = 4
FFN = 4 * D
EPS = 1e-6
LANES = 128
SUBLANES = 8

OFF_I = 6 * D
OFF_GA = OFF_I + 2 * HEADS
N_SEG = 8
SEG_U, SEG_V, SEG_Q, SEG_K, SEG_MV, SEG_O, SEG_GA, SEG_GB = range(N_SEG)
CW = 256
NCH = D // CW

TM_IN = 512
L_MIX = 256
ROWS_MIX = 2
TM_FFN = 512
FFN_CHUNK = 1024
ADA_COLS = 1536

F32 = jnp.float32
BF16 = jnp.bfloat16
VMEM_LIMIT = 56 * 1024 * 1024


def _resident(shape):
    nd = len(shape)
    return pl.BlockSpec(shape, lambda *_: (0,) * nd, pipeline_mode=pl.Buffered(1))


def _ada_kernel(c_ref, w_ref, b_ref, o_ref):
    c = c_ref[...]
    ca = c * jax.nn.sigmoid(c)
    o_ref[...] = jnp.dot(ca, w_ref[...], preferred_element_type=F32,
                         precision=lax.Precision.HIGHEST) + b_ref[...]


def _ada(c, w, b):
    n = w.shape[1]
    return pl.pallas_call(
        _ada_kernel,
        grid=(n // ADA_COLS,),
        in_specs=[pl.BlockSpec((BATCH, D), lambda j: (0, 0)),
                  pl.BlockSpec((D, ADA_COLS), lambda j: (0, j)),
                  pl.BlockSpec((1, ADA_COLS), lambda j: (0, j))],
        out_specs=pl.BlockSpec((BATCH, ADA_COLS), lambda j: (0, j)),
        out_shape=jax.ShapeDtypeStruct((BATCH, n), F32),
        compiler_params=pltpu.CompilerParams(vmem_limit_bytes=VMEM_LIMIT),
        name="ada",
    )(c, w, b.reshape(1, n))


def _wprep_kernel(a_ref, b_ref, o_ref):
    s = pl.program_id(0)

    def emit(rows):
        for c in range(NCH):
            o_ref[c] = rows(c * CW, (c + 1) * CW).T.astype(BF16)

    @pl.when(s < SEG_GA)
    def _():
        emit(lambda lo, hi: a_ref[lo:hi, :])

    @pl.when(s >= SEG_GA)
    def _():
        skip = 2 * HEADS
        emit(lambda lo, hi: a_ref[lo + skip:hi + skip, :] if hi < D else
             jnp.concatenate([a_ref[lo + skip:, :], b_ref[...]], axis=0))


def _wprep(wt):
    gate_rows = 2 * HEADS
    return pl.pallas_call(
        _wprep_kernel,
        grid=(N_SEG,),
        in_specs=[pl.BlockSpec((D, D), lambda s: (s, 0)),
                  pl.BlockSpec((gate_rows, D), lambda s: ((s + 1) * (D // gate_rows), 0))],
        out_specs=pl.BlockSpec((None, NCH, D, CW), lambda s: (s, 0, 0, 0)),
        out_shape=jax.ShapeDtypeStruct((N_SEG, NCH, D, CW), BF16),
        compiler_params=pltpu.CompilerParams(vmem_limit_bytes=VMEM_LIMIT),
        name="wprep",
    )(wt, wt)


def _rms_mod(x, g, shift, scale):
    y = x * lax.rsqrt(jnp.mean(x * x, axis=-1, keepdims=True) + EPS)
    return y * (g * (1.0 + scale)) + shift


def _conv_silu(z, halo_ref, c, w, b):
    assert CONV_K == 4 and w.shape[0] == CONV_K
    tm = z.shape[0]
    zz = jnp.concatenate([halo_ref[c], z], axis=0)
    z1 = pltpu.roll(zz, 1, axis=0)
    older = pltpu.roll(w[1:2, :] * zz + w[0:1, :] * z1, 2, axis=0)
    acc = (b + w[3:4, :] * z + w[2:3, :] * z1[SUBLANES:, :]) + older[SUBLANES:, :]
    halo_ref[c] = z[tm - SUBLANES:, :]
    return acc * jax.nn.sigmoid(acc)


def _inproj_kernel(x_ref, sh_ref, sc_ref, g_ref, w_ref, wg_ref, gb_ref, cw_ref, cb_ref,
                   lng_ref, lnb_ref,
                   u_ref, v_ref, q_ref, kt_ref, mv_ref, o_ref, ga_ref, gbo_ref, gate_ref,
                   xn_scr, gv_scr, s1_scr, halo_scr):
    i = pl.program_id(0)

    @pl.when((i % (SEQ // TM_IN)) == 0)
    def _():
        halo_scr[...] = jnp.zeros(halo_scr.shape, F32)

    xn_scr[...] = _rms_mod(x_ref[...], g_ref[...], sh_ref[...], sc_ref[...]).astype(BF16)
    s1_scr[...] = jnp.zeros(s1_scr.shape, F32)

    def seg(k, c):
        return jnp.dot(xn_scr[...], w_ref[k, c], preferred_element_type=F32)

    for c in range(NCH):
        k = _conv_silu(seg(SEG_K, c), halo_scr.at[1], c, cw_ref[1, c], cb_ref[1, c])
        kt_ref[c] = k.T.astype(BF16)
        mv_ref[c] = seg(SEG_MV, c).astype(BF16)
        q = _conv_silu(seg(SEG_Q, c), halo_scr.at[0], c, cw_ref[0, c], cb_ref[0, c])
        q_ref[c] = (q * (HEAD_DIM ** -0.5)).astype(BF16)
        o_ref[c] = jax.nn.sigmoid(seg(SEG_O, c)).astype(BF16)
        gv = jax.nn.gelu(seg(SEG_V, c))
        gv_scr[c] = gv
        s1_scr[...] += gv[:, :LANES] + gv[:, LANES:]
        ga_ref[c] = jax.nn.sigmoid(seg(SEG_GA, c)).astype(BF16)
        u_ref[c] = jax.nn.gelu(seg(SEG_U, c)).astype(BF16)
        gbo_ref[c] = jax.nn.sigmoid(seg(SEG_GB, c)).astype(BF16)

    wg = jnp.concatenate([wg_ref[...], jnp.zeros((LANES - 2 * HEADS, D), F32)], axis=0)
    gp = lax.dot_general(xn_scr[...], wg.astype(BF16), (((1,), (1,)), ((), ())),
                         preferred_element_type=F32) + gb_ref[...]
    lane = lax.broadcasted_iota(jnp.int32, gp.shape, 1)
    gate_ref[...] = jnp.where(lane < HEADS, gp, jax.nn.log_sigmoid(gp))

    mu = jnp.sum(s1_scr[...], axis=-1, keepdims=True) * (1.0 / D)
    ssq = jnp.zeros((TM_IN, LANES), F32)
    for c in range(NCH):
        dv = gv_scr[c] - mu
        gv_scr[c] = dv
        dv = dv * dv
        ssq = ssq + (dv[:, :LANES] + dv[:, LANES:])
    rstd = lax.rsqrt(jnp.sum(ssq, axis=-1, keepdims=True) * (1.0 / D) + EPS)
    for c in range(NCH):
        v_ref[c] = ((gv_scr[c] * rstd) * lng_ref[c] + lnb_ref[c]).astype(BF16)


def _inproj(x2, sh1, sc1, norm_g, w_all, w_gate, gate_bias, conv_w, conv_b, ln_g, ln_b):
    nsb = SEQ // TM_IN
    row = lambda i: (i, 0)
    per_batch = lambda i: (i // nsb, 0, 0)
    seg_out = pl.BlockSpec((NCH, TM_IN, CW), lambda i: (0, i, 0))
    seg_shape = jax.ShapeDtypeStruct((NCH, TOKENS, CW), BF16)
    seg_outs = [seg_out] * N_SEG
    seg_shapes = [seg_shape] * N_SEG
    seg_outs[SEG_K] = pl.BlockSpec((NCH, CW, TM_IN), lambda i: (0, 0, i))
    seg_shapes[SEG_K] = jax.ShapeDtypeStruct((NCH, CW, TOKENS), BF16)
    return pl.pallas_call(
        _inproj_kernel,
        grid=(TOKENS // TM_IN,),
        in_specs=[pl.BlockSpec((TM_IN, D), row),
                  pl.BlockSpec((None, 1, D), per_batch),
                  pl.BlockSpec((None, 1, D), per_batch),
                  _resident((1, D)),
                  _resident((N_SEG, NCH, D, CW)),
                  _resident((2 * HEADS, D)),
                  _resident((1, LANES)),
                  _resident((2, NCH, CONV_K, CW)),
                  _resident((2, NCH, 1, CW)),
                  _resident((NCH, 1, CW)),
                  _resident((NCH, 1, CW))],
        out_specs=seg_outs + [pl.BlockSpec((TM_IN, LANES), row)],
        out_shape=seg_shapes + [jax.ShapeDtypeStruct((TOKENS, LANES), F32)],
        scratch_shapes=[pltpu.VMEM((TM_IN, D), BF16),
                        pltpu.VMEM((NCH, TM_IN, CW), F32),
                        pltpu.VMEM((TM_IN, LANES), F32),
                        pltpu.VMEM((2, NCH, SUBLANES, CW), F32)],
        compiler_params=pltpu.CompilerParams(dimension_semantics=("arbitrary",),
                                             vmem_limit_bytes=VMEM_LIMIT),
        name="inproj",
    )(x2, sh1, sc1, norm_g, w_all, w_gate, gate_bias, conv_w, conv_b, ln_g, ln_b)


def _cumsum_rows(x):
    n = x.shape[0]
    row = lax.broadcasted_iota(jnp.int32, x.shape, 0)
    k = 1
    while k < n:
        x = x + jnp.where(row >= k, pltpu.roll(x, k, axis=0), 0.0)
        k *= 2
    return x


def _rep2(a):
    return jnp.concatenate([a, a], axis=1)


def _mixer_kernel(x_ref, g1_ref, u_ref, v_ref, q_ref, kt0_ref, kt1_ref, mv_ref, o_ref, ga_ref,
                  gb_ref, gate_ref, ws_ref, bsb_ref, hng_ref, wout_ref,
                  h1_ref,
                  c_scr, n_scr, m_scr, merged_scr):
    L = L_MIX
    j = pl.program_id(1)

    @pl.when(j == 0)
    def _():
        c_scr[...] = jnp.zeros(c_scr.shape, F32)
        n_scr[...] = jnp.zeros(n_scr.shape, F32)
        m_scr[...] = jnp.zeros(m_scr.shape, F32)

    kt_refs = (kt0_ref, kt1_ref)
    scans = []
    for r in range(ROWS_MIX):
        gates = gate_ref[r]
        bcum = _cumsum_rows(gates)
        scans.append((gates.T, bcum, bcum.T))
    row = lax.broadcasted_iota(jnp.int32, (L, L), 0)
    col = lax.broadcasted_iota(jnp.int32, (L, L), 1)
    causal = col <= row
    ones_rhs = jnp.ones((L, LANES), BF16)

    brow = lax.broadcasted_iota(jnp.int32, (GMLP_BLOCK, GMLP_BLOCK), 0) // CHUNK
    bcol = lax.broadcasted_iota(jnp.int32, (GMLP_BLOCK, GMLP_BLOCK), 1) // CHUNK
    wmask = bcol <= brow
    nblk = L // GMLP_BLOCK

    for h in range(HEADS):
        hs = slice(h * HEAD_DIM, (h + 1) * HEAD_DIM)
        for r in range(ROWS_MIX):
            gates_t, bcum, bcum_t = scans[r]
            li_r = gates_t[h:h + 1, :]
            b_r = bcum_t[HEADS + h:HEADS + h + 1, :]
            b_c = jnp.broadcast_to(bcum[:, HEADS + h:HEADS + h + 1], (L, LANES))
            m_prev = m_scr[r, h, 0:1, 0:1]
            dmat = jnp.where(causal, _rep2(b_c) + (li_r - b_r), -jnp.inf)
            m_inter = b_c + m_prev
            m_t = jnp.maximum(m_inter, jnp.max(dmat, axis=-1, keepdims=True))
            wts = jnp.exp(dmat - _rep2(m_t))
            qh = q_ref[h, r]
            kt = kt_refs[r][h]
            vh = mv_ref[h, r]
            s = jnp.dot(qh, kt, preferred_element_type=F32) * wts
            sb = s.astype(BF16)
            inter = jnp.exp(m_inter - m_t)
            c_old = c_scr[r, h]
            n_old = n_scr[r, h]
            n_hi = n_old.astype(BF16)
            n_lo = (n_old - n_hi.astype(F32)).astype(BF16)
            qn2 = jnp.dot(qh, jnp.concatenate([n_hi, n_lo], axis=1), preferred_element_type=F32)
            qn = qn2[:, :LANES] + qn2[:, LANES:]
            num = (jnp.dot(sb, vh, preferred_element_type=F32)
                   + _rep2(inter) * jnp.dot(qh, c_old.astype(BF16), preferred_element_type=F32))
            den = jnp.dot(sb, ones_rhs, preferred_element_type=F32) + inter * qn
            inv = 1.0 / jnp.maximum(jnp.abs(den), jnp.exp(-m_t))
            hh = num * _rep2(inv)

            b_last = b_r[:, L - 1:L]
            g_r = (b_last - b_r) + li_r
            m_new = jnp.maximum(b_last + m_prev, jnp.max(g_r, axis=-1, keepdims=True))
            decay = jnp.exp((b_last + m_prev) - m_new)
            wkt = jnp.exp(g_r - m_new).astype(BF16) * kt
            c_scr[r, h] = decay * c_old + jnp.dot(wkt, vh, preferred_element_type=F32)
            n_scr[r, h] = decay * n_old + jnp.dot(wkt, ones_rhs, preferred_element_type=F32)
            m_scr[r, h] = jnp.broadcast_to(m_new, (SUBLANES, LANES))

            mu = jnp.mean(hh, axis=-1, keepdims=True)
            dh = hh - mu
            var = jnp.mean(dh * dh, axis=-1, keepdims=True)
            y_n = (dh * lax.rsqrt(var + EPS)) * hng_ref[:, hs]

            for half in range(CW // GROUP_DIM):
                g = h * (CW // GROUP_DIM) + half
                ls = slice(half * GROUP_DIM, (half + 1) * GROUP_DIM)
                wsg = jnp.where(wmask, ws_ref[g], 0.0).astype(BF16)
                vcat = jnp.concatenate(
                    [v_ref[h, r, n * GMLP_BLOCK:(n + 1) * GMLP_BLOCK, ls] for n in range(nblk)],
                    axis=1)
                mixed = jnp.dot(wsg, vcat, preferred_element_type=F32) + _rep2(bsb_ref[g])
                for n in range(nblk):
                    rs = slice(n * GMLP_BLOCK, (n + 1) * GMLP_BLOCK)
                    gate_a = (ga_ref[h, r, rs, ls] * u_ref[h, r, rs, ls]).astype(F32)
                    gate_b = (gb_ref[h, r, rs, ls] * o_ref[h, r, rs, ls]).astype(F32)
                    merged = (gate_a * mixed[:, n * GMLP_BLOCK:(n + 1) * GMLP_BLOCK]
                              + gate_b * y_n[rs, ls])
                    merged_scr[r * L + n * GMLP_BLOCK:r * L + (n + 1) * GMLP_BLOCK,
                               g * GROUP_DIM:(g + 1) * GROUP_DIM] = merged.astype(BF16)

    proj = jnp.dot(merged_scr[...], wout_ref[...], preferred_element_type=F32)
    for r in range(ROWS_MIX):
        h1_ref[r] = x_ref[r] + g1_ref[r] * proj[r * L:(r + 1) * L, :]


def _mixer(x3, g1, u, v, q, kt, mv, o, ga, gb, gates, ws, bs_rep, hn_g, w_out):
    nsb = SEQ // L_MIX
    rows3 = pl.BlockSpec((ROWS_MIX, L_MIX, D), lambda b, j: (b, j, 0))
    seg_in = pl.BlockSpec((NCH, ROWS_MIX, L_MIX, CW), lambda b, j: (0, b, j, 0))
    kt_in = [pl.BlockSpec((NCH, CW, L_MIX), lambda b, j, r=r: (0, 0, (ROWS_MIX * b + r) * nsb + j))
             for r in range(ROWS_MIX)]
    seg4 = lambda a: a.reshape(NCH, BATCH, SEQ, CW)
    return pl.pallas_call(
        _mixer_kernel,
        grid=(BATCH // ROWS_MIX, nsb),
        in_specs=[rows3,
                  pl.BlockSpec((ROWS_MIX, 1, D), lambda b, j: (b, 0, 0)),
                  seg_in, seg_in, seg_in, kt_in[0], kt_in[1], seg_in, seg_in, seg_in, seg_in,
                  pl.BlockSpec((ROWS_MIX, L_MIX, LANES), lambda b, j: (b, j, 0)),
                  _resident((GMLP_GROUPS, GMLP_BLOCK, GMLP_BLOCK)),
                  _resident((GMLP_GROUPS, GMLP_BLOCK, LANES)),
                  _resident((1, D)),
                  _resident((D, D))],
        out_specs=rows3,
        out_shape=jax.ShapeDtypeStruct((BATCH, SEQ, D), F32),
        scratch_shapes=[pltpu.VMEM((ROWS_MIX, HEADS, HEAD_DIM, HEAD_DIM), F32),
                        pltpu.VMEM((ROWS_MIX, HEADS, HEAD_DIM, LANES), F32),
                        pltpu.VMEM((ROWS_MIX, HEADS, SUBLANES, LANES), F32),
                        pltpu.VMEM((ROWS_MIX * L_MIX, D), BF16)],
        compiler_params=pltpu.CompilerParams(dimension_semantics=("arbitrary", "arbitrary"),
                                             vmem_limit_bytes=VMEM_LIMIT),
        name="mixer",
    )(x3, g1, seg4(u), seg4(v), seg4(q), kt, kt, seg4(mv), seg4(o), seg4(ga), seg4(gb),
      gates.reshape(BATCH, SEQ, LANES), ws, bs_rep, hn_g, w_out)


def _ffn_kernel(h_ref, sh_ref, sc_ref, g2_ref, ng_ref, w1_ref, w2_ref, fg_ref, o_ref):
    h = h_ref[...]
    xn = _rms_mod(h, ng_ref[...], sh_ref[...], sc_ref[...]).astype(BF16)
    acc = jnp.zeros((TM_FFN, D), F32)
    for c in range(FFN // FFN_CHUNK):
        cs = slice(c * FFN_CHUNK, (c + 1) * FFN_CHUNK)
        a = jnp.maximum(jnp.dot(xn, w1_ref[:, cs], preferred_element_type=F32), 0.0)
        acc = acc + jnp.dot((a * a).astype(BF16), w2_ref[cs, :], preferred_element_type=F32)
    h2 = h + g2_ref[...] * acc
    y = h2 * lax.rsqrt(jnp.mean(h2 * h2, axis=-1, keepdims=True) + EPS)
    o_ref[...] = y * fg_ref[...]


def _ffn(h1, sh2, sc2, g2, norm_g, w1, w2, final_g):
    nsb = SEQ // TM_FFN
    row = lambda i: (i, 0)
    per_batch = lambda i: (i // nsb, 0, 0)
    return pl.pallas_call(
        _ffn_kernel,
        grid=(TOKENS // TM_FFN,),
        in_specs=[pl.BlockSpec((TM_FFN, D), row),
                  pl.BlockSpec((None, 1, D), per_batch),
                  pl.BlockSpec((None, 1, D), per_batch),
                  pl.BlockSpec((None, 1, D), per_batch),
                  _resident((1, D)),
                  _resident((D, FFN)),
                  _resident((FFN, D)),
                  _resident((1, D))],
        out_specs=pl.BlockSpec((TM_FFN, D), row),
        out_shape=jax.ShapeDtypeStruct((TOKENS, D), F32),
        compiler_params=pltpu.CompilerParams(dimension_semantics=("arbitrary",),
                                             vmem_limit_bytes=VMEM_LIMIT),
        name="ffn",
    )(h1, sh2, sc2, g2, norm_g, w1, w2, final_g)


def kernel(x, c, w_ada, b_ada, norm1_g, w_in, conv_w, conv_b, mlstm_gate_b, gmlp_ln_g, gmlp_ln_b,
           gmlp_ws, gmlp_bs, mlstm_hn_g, w_out, norm2_g, w_ff1, w_ff2, final_g):
    l = 0
    mod = _ada(c, w_ada[l], b_ada[l]).reshape(BATCH, 6, 1, D)
    sh1, sc1, g1, sh2, sc2, g2 = (mod[:, t] for t in range(6))

    wt = jnp.swapaxes(w_in, 1, 2)[l]
    w_all = _wprep(wt)
    w_gate = wt[OFF_I:OFF_GA]
    gate_bias = jnp.pad(mlstm_gate_b[l].reshape(1, 2 * HEADS), ((0, 0), (0, LANES - 2 * HEADS)))
    cw = conv_w[l].reshape(CONV_K, 2, NCH, CW).transpose(1, 2, 0, 3)
    cb = conv_b[l].reshape(2, NCH, 1, CW)

    x2 = x.reshape(TOKENS, D)
    u, v, q, kt, mv, o, ga, gb, gates = _inproj(
        x2, sh1, sc1, norm1_g[l].reshape(1, D), w_all, w_gate, gate_bias, cw, cb,
        gmlp_ln_g[l].reshape(NCH, 1, CW), gmlp_ln_b[l].reshape(NCH, 1, CW))

    bs_rep = jnp.broadcast_to(gmlp_bs[l][:, :, None], (GMLP_GROUPS, GMLP_BLOCK, LANES))
    h1 = _mixer(x, g1, u, v, q, kt, mv, o, ga, gb, gates,
                gmlp_ws[l], bs_rep, mlstm_hn_g[l].reshape(1, D), w_out[l].astype(BF16))
    h1 = h1.reshape(TOKENS, D)

    out = _ffn(h1, sh2, sc2, g2, norm2_g[l].reshape(1, D),
               w_ff1[l].astype(BF16), w_ff2[l].astype(BF16), final_g.reshape(1, D))
    return out.reshape(BATCH, SEQ, D)
```

```python
import jax
import jax.numpy as jnp
from jax import lax
from jax.experimental import pallas as pl
from jax.experimental.pallas import tpu as pltpu

D = 1024
BATCH = 16
SEQ = 2048
TOKENS = BATCH * SEQ
CHUNK = 64
GMLP_BLOCK = 128
GMLP_GROUPS = 8
GROUP_DIM = D // GMLP_GROUPS
HEADS = 4
HEAD_DIM = 256
CONV_K = 4
FFN = 4 * D
EPS = 1e-6
LANES = 128
SUBLANES = 8

OFF_I = 6 * D
OFF_GA = OFF_I + 2 * HEADS
N_SEG = 8
SEG_U, SEG_V, SEG_Q, SEG_K, SEG_MV, SEG_O, SEG_GA, SEG_GB = range(N_SEG)
CW = 256
NCH = D // CW

TM_IN = 512
L_MIX = 256
ROWS_MIX = 2
TM_FFN = 512
FFN_CHUNK = 1024
ADA_COLS = 1536

F32 = jnp.float32
BF16 = jnp.bfloat16
VMEM_LIMIT = 56 * 1024 * 1024


def _resident(shape):
    nd = len(shape)
    return pl.BlockSpec(shape, lambda *_: (0,) * nd, pipeline_mode=pl.Buffered(1))


def _hi_lo(a):
    hi = a.astype(BF16)
    return hi, (a - hi.astype(F32)).astype(BF16)


def _ada_kernel(c_ref, w_ref, b_ref, o_ref):
    c = c_ref[...]
    ca_hi, ca_lo = _hi_lo(c * jax.nn.sigmoid(c))
    w_hi, w_lo = _hi_lo(w_ref[...])
    p = jnp.dot(jnp.concatenate([ca_hi, ca_lo], axis=0), w_hi, preferred_element_type=F32)
    o_ref[...] = (p[:BATCH] + p[BATCH:]
                  + jnp.dot(ca_hi, w_lo, preferred_element_type=F32) + b_ref[...])


def _ada(c, w, b):
    n = w.shape[1]
    return pl.pallas_call(
        _ada_kernel,
        grid=(n // ADA_COLS,),
        in_specs=[pl.BlockSpec((BATCH, D), lambda j: (0, 0)),
                  pl.BlockSpec((D, ADA_COLS), lambda j: (0, j)),
                  pl.BlockSpec((1, ADA_COLS), lambda j: (0, j))],
        out_specs=pl.BlockSpec((BATCH, ADA_COLS), lambda j: (0, j)),
        out_shape=jax.ShapeDtypeStruct((BATCH, n), F32),
        compiler_params=pltpu.CompilerParams(vmem_limit_bytes=VMEM_LIMIT),
        name="ada",
    )(c, w, b.reshape(1, n))


def _wprep_kernel(a_ref, b_ref, o_ref):
    s = pl.program_id(0)

    def emit(rows):
        for c in range(NCH):
            o_ref[c] = rows(c * CW, (c + 1) * CW).T.astype(BF16)

    @pl.when(s < SEG_GA)
    def _():
        emit(lambda lo, hi: a_ref[lo:hi, :])

    @pl.when(s >= SEG_GA)
    def _():
        skip = 2 * HEADS
        emit(lambda lo, hi: a_ref[lo + skip:hi + skip, :] if hi < D else
             jnp.concatenate([a_ref[lo + skip:, :], b_ref[...]], axis=0))


def _wprep(wt):
    gate_rows = 2 * HEADS
    return pl.pallas_call(
        _wprep_kernel,
        grid=(N_SEG,),
        in_specs=[pl.BlockSpec((D, D), lambda s: (s, 0)),
                  pl.BlockSpec((gate_rows, D), lambda s: ((s + 1) * (D // gate_rows), 0))],
        out_specs=pl.BlockSpec((None, NCH, D, CW), lambda s: (s, 0, 0, 0)),
        out_shape=jax.ShapeDtypeStruct((N_SEG, NCH, D, CW), BF16),
        compiler_params=pltpu.CompilerParams(vmem_limit_bytes=VMEM_LIMIT),
        name="wprep",
    )(wt, wt)


def _rms_mod(x, g, shift, scale):
    y = x * lax.rsqrt(jnp.mean(x * x, axis=-1, keepdims=True) + EPS)
    return y * (g * (1.0 + scale)) + shift


def _conv_silu(z, halo_ref, c, w, b):
    assert CONV_K == 4 and w.shape[0] == CONV_K
    tm = z.shape[0]
    zz = jnp.concatenate([halo_ref[c], z], axis=0)
    z1 = pltpu.roll(zz, 1, axis=0)
    older = pltpu.roll(w[1:2, :] * zz + w[0:1, :] * z1, 2, axis=0)
    acc = (b + w[3:4, :] * z + w[2:3, :] * z1[SUBLANES:, :]) + older[SUBLANES:, :]
    halo_ref[c] = z[tm - SUBLANES:, :]
    return acc * jax.nn.sigmoid(acc)


def _inproj_kernel(x_ref, sh_ref, sc_ref, g_ref, w_ref, wg_ref, gb_ref, cw_ref, cb_ref,
                   lng_ref, lnb_ref,
                   ua_ref, v_ref, q_ref, kt_ref, mv_ref, og_ref, gate_ref,
                   xn_scr, gv_scr, s1_scr, halo_scr):
    i = pl.program_id(0)

    @pl.when((i % (SEQ // TM_IN)) == 0)
    def _():
        halo_scr[...] = jnp.zeros(halo_scr.shape, F32)

    xn_scr[...] = _rms_mod(x_ref[...], g_ref[...], sh_ref[...], sc_ref[...]).astype(BF16)
    s1_scr[...] = jnp.zeros(s1_scr.shape, F32)

    def seg(k, c):
        return jnp.dot(xn_scr[...], w_ref[k, c], preferred_element_type=F32)

    for c in range(NCH):
        k = _conv_silu(seg(SEG_K, c), halo_scr.at[1], c, cw_ref[1, c], cb_ref[1, c])
        kt_ref[c] = k.T.astype(BF16)
        sig_o = jax.nn.sigmoid(seg(SEG_O, c))
        q = _conv_silu(seg(SEG_Q, c), halo_scr.at[0], c, cw_ref[0, c], cb_ref[0, c])
        q_ref[c] = (q * (HEAD_DIM ** -0.5)).astype(BF16)
        og_ref[c] = (jax.nn.sigmoid(seg(SEG_GB, c)) * sig_o).astype(BF16)
        gv = jax.nn.gelu(seg(SEG_V, c))
        gv_scr[c] = gv
        s1_scr[...] += gv[:, :LANES] + gv[:, LANES:]
        sig_ga = jax.nn.sigmoid(seg(SEG_GA, c))
        ua_ref[c] = (jax.nn.gelu(seg(SEG_U, c)) * sig_ga).astype(BF16)
        mv_ref[c] = seg(SEG_MV, c).astype(BF16)

    wg = jnp.concatenate([wg_ref[...], jnp.zeros((LANES - 2 * HEADS, D), F32)], axis=0)
    gp = lax.dot_general(xn_scr[...], wg.astype(BF16), (((1,), (1,)), ((), ())),
                         preferred_element_type=F32) + gb_ref[...]
    lane = lax.broadcasted_iota(jnp.int32, gp.shape, 1)
    gate_ref[...] = jnp.where(lane < HEADS, gp, jax.nn.log_sigmoid(gp))

    mu = jnp.sum(s1_scr[...], axis=-1, keepdims=True) * (1.0 / D)
    ssq = jnp.zeros((TM_IN, LANES), F32)
    for c in range(NCH):
        dv = gv_scr[c] - mu
        gv_scr[c] = dv
        dv = dv * dv
        ssq = ssq + (dv[:, :LANES] + dv[:, LANES:])
    rstd = lax.rsqrt(jnp.sum(ssq, axis=-1, keepdims=True) * (1.0 / D) + EPS)
    for c in range(NCH):
        v_ref[c] = ((gv_scr[c] * rstd) * lng_ref[c] + lnb_ref[c]).astype(BF16)


def _inproj(x2, sh1, sc1, norm_g, w_all, w_gate, gate_bias, conv_w, conv_b, ln_g, ln_b):
    nsb = SEQ // TM_IN
    row = lambda i: (i, 0)
    per_batch = lambda i: (i // nsb, 0, 0)
    seg_out = pl.BlockSpec((NCH, TM_IN, CW), lambda i: (0, i, 0))
    seg_shape = jax.ShapeDtypeStruct((NCH, TOKENS, CW), BF16)
    seg_outs = [seg_out, seg_out, seg_out,
                pl.BlockSpec((NCH, CW, TM_IN), lambda i: (0, 0, i)), seg_out, seg_out]
    seg_shapes = [seg_shape, seg_shape, seg_shape,
                  jax.ShapeDtypeStruct((NCH, CW, TOKENS), BF16), seg_shape, seg_shape]
    return pl.pallas_call(
        _inproj_kernel,
        grid=(TOKENS // TM_IN,),
        in_specs=[pl.BlockSpec((TM_IN, D), row),
                  pl.BlockSpec((None, 1, D), per_batch),
                  pl.BlockSpec((None, 1, D), per_batch),
                  _resident((1, D)),
                  _resident((N_SEG, NCH, D, CW)),
                  _resident((2 * HEADS, D)),
                  _resident((1, LANES)),
                  _resident((2, NCH, CONV_K, CW)),
                  _resident((2, NCH, 1, CW)),
                  _resident((NCH, 1, CW)),
                  _resident((NCH, 1, CW))],
        out_specs=seg_outs + [pl.BlockSpec((TM_IN, LANES), row)],
        out_shape=seg_shapes + [jax.ShapeDtypeStruct((TOKENS, LANES), F32)],
        scratch_shapes=[pltpu.VMEM((TM_IN, D), BF16),
                        pltpu.VMEM((NCH, TM_IN, CW), F32),
                        pltpu.VMEM((TM_IN, LANES), F32),
                        pltpu.VMEM((2, NCH, SUBLANES, CW), F32)],
        compiler_params=pltpu.CompilerParams(dimension_semantics=("arbitrary",),
                                             vmem_limit_bytes=VMEM_LIMIT),
        name="inproj",
    )(x2, sh1, sc1, norm_g, w_all, w_gate, gate_bias, conv_w, conv_b, ln_g, ln_b)


def _cumsum_rows(x):
    n = x.shape[0]
    row = lax.broadcasted_iota(jnp.int32, x.shape, 0)
    k = 1
    while k < n:
        x = x + jnp.where(row >= k, pltpu.roll(x, k, axis=0), 0.0)
        k *= 2
    return x


def _rep2(a):
    return jnp.concatenate([a, a], axis=1)


def _mixer_kernel(x_ref, g1_ref, ua_ref, v_ref, q_ref, kt0_ref, kt1_ref, mv_ref, og_ref,
                  gate_ref, ws_ref, bsb_ref, hng_ref, wout_ref,
                  h1_ref,
                  c_scr, n_scr, m_scr, merged_scr):
    L = L_MIX
    j = pl.program_id(1)

    @pl.when(j == 0)
    def _():
        c_scr[...] = jnp.zeros(c_scr.shape, F32)
        n_scr[...] = jnp.zeros(n_scr.shape, F32)
        m_scr[...] = jnp.zeros(m_scr.shape, F32)

    kt_refs = (kt0_ref, kt1_ref)
    scans = []
    for r in range(ROWS_MIX):
        gates = gate_ref[r]
        bcum = _cumsum_rows(gates)
        scans.append((gates.T, bcum, bcum.T))
    row = lax.broadcasted_iota(jnp.int32, (L, L), 0)
    col = lax.broadcasted_iota(jnp.int32, (L, L), 1)
    causal = col <= row
    ones_rhs = jnp.ones((L, LANES), BF16)

    brow = lax.broadcasted_iota(jnp.int32, (GMLP_BLOCK, GMLP_BLOCK), 0) // CHUNK
    bcol = lax.broadcasted_iota(jnp.int32, (GMLP_BLOCK, GMLP_BLOCK), 1) // CHUNK
    wmask = bcol <= brow
    nblk = L // GMLP_BLOCK

    for h in range(HEADS):
        hs = slice(h * HEAD_DIM, (h + 1) * HEAD_DIM)
        for r in range(ROWS_MIX):
            gates_t, bcum, bcum_t = scans[r]
            li_r = gates_t[h:h + 1, :]
            b_r = bcum_t[HEADS + h:HEADS + h + 1, :]
            b_c = jnp.broadcast_to(bcum[:, HEADS + h:HEADS + h + 1], (L, LANES))
            m_prev = m_scr[r, h, 0:1, 0:1]
            dmat = jnp.where(causal, _rep2(b_c) + (li_r - b_r), -jnp.inf)
            m_inter = b_c + m_prev
            m_t = jnp.maximum(m_inter, jnp.max(dmat, axis=-1, keepdims=True))
            wts = jnp.exp(dmat - _rep2(m_t))
            qh = q_ref[h, r]
            kt = kt_refs[r][h]
            vh = mv_ref[h, r]
            s = jnp.dot(qh, kt, preferred_element_type=F32) * wts
            sb = s.astype(BF16)
            inter = jnp.exp(m_inter - m_t)
            c_old = c_scr[r, h]
            n_old = n_scr[r, h]
            n_hi = n_old.astype(BF16)
            n_lo = (n_old - n_hi.astype(F32)).astype(BF16)
            qn2 = jnp.dot(qh, jnp.concatenate([n_hi, n_lo], axis=1), preferred_element_type=F32)
            qn = qn2[:, :LANES] + qn2[:, LANES:]
            num = (jnp.dot(sb, vh, preferred_element_type=F32)
                   + _rep2(inter) * jnp.dot(qh, c_old.astype(BF16), preferred_element_type=F32))
            den = jnp.dot(sb, ones_rhs, preferred_element_type=F32) + inter * qn
            inv = 1.0 / jnp.maximum(jnp.abs(den), jnp.exp(-m_t))
            hh = num * _rep2(inv)

            b_last = b_r[:, L - 1:L]
            g_r = (b_last - b_r) + li_r
            m_new = jnp.maximum(b_last + m_prev, jnp.max(g_r, axis=-1, keepdims=True))
            decay = jnp.exp((b_last + m_prev) - m_new)
            wkt = jnp.exp(g_r - m_new).astype(BF16) * kt
            c_scr[r, h] = decay * c_old + jnp.dot(wkt, vh, preferred_element_type=F32)
            n_scr[r, h] = decay * n_old + jnp.dot(wkt, ones_rhs, preferred_element_type=F32)
            m_scr[r, h] = jnp.broadcast_to(m_new, (SUBLANES, LANES))

            mu = jnp.mean(hh, axis=-1, keepdims=True)
            dh = hh - mu
            var = jnp.mean(dh * dh, axis=-1, keepdims=True)
            y_n = (dh * lax.rsqrt(var + EPS)) * hng_ref[:, hs]

            for half in range(CW // GROUP_DIM):
                g = h * (CW // GROUP_DIM) + half
                ls = slice(half * GROUP_DIM, (half + 1) * GROUP_DIM)
                wsg = jnp.where(wmask, ws_ref[g], 0.0).astype(BF16)
                vcat = jnp.concatenate(
                    [v_ref[h, r, n * GMLP_BLOCK:(n + 1) * GMLP_BLOCK, ls] for n in range(nblk)],
                    axis=1)
                mixed = jnp.dot(wsg, vcat, preferred_element_type=F32) + _rep2(bsb_ref[g])
                for n in range(nblk):
                    rs = slice(n * GMLP_BLOCK, (n + 1) * GMLP_BLOCK)
                    gate_a = ua_ref[h, r, rs, ls].astype(F32)
                    gate_b = og_ref[h, r, rs, ls].astype(F32)
                    merged = (gate_a * mixed[:, n * GMLP_BLOCK:(n + 1) * GMLP_BLOCK]
                              + gate_b * y_n[rs, ls])
                    merged_scr[r * L + n * GMLP_BLOCK:r * L + (n + 1) * GMLP_BLOCK,
                               g * GROUP_DIM:(g + 1) * GROUP_DIM] = merged.astype(BF16)

    proj = jnp.dot(merged_scr[...], wout_ref[...], preferred_element_type=F32)
    for r in range(ROWS_MIX):
        h1_ref[r] = x_ref[r] + g1_ref[r] * proj[r * L:(r + 1) * L, :]


def _mixer(x3, g1, ua, v, q, kt, mv, og, gates, ws, bs_rep, hn_g, w_out):
    nsb = SEQ // L_MIX
    rows3 = pl.BlockSpec((ROWS_MIX, L_MIX, D), lambda b, j: (b, j, 0))
    seg_in = pl.BlockSpec((NCH, ROWS_MIX, L_MIX, CW), lambda b, j: (0, b, j, 0))
    kt_in = [pl.BlockSpec((NCH, CW, L_MIX), lambda b, j, r=r: (0, 0, (ROWS_MIX * b + r) * nsb + j))
             for r in range(ROWS_MIX)]
    seg4 = lambda a: a.reshape(NCH, BATCH, SEQ, CW)
    return pl.pallas_call(
        _mixer_kernel,
        grid=(BATCH // ROWS_MIX, nsb),
        in_specs=[rows3,
                  pl.BlockSpec((ROWS_MIX, 1, D), lambda b, j: (b, 0, 0)),
                  seg_in, seg_in, seg_in, kt_in[0], kt_in[1], seg_in, seg_in,
                  pl.BlockSpec((ROWS_MIX, L_MIX, LANES), lambda b, j: (b, j, 0)),
                  _resident((GMLP_GROUPS, GMLP_BLOCK, GMLP_BLOCK)),
                  _resident((GMLP_GROUPS, GMLP_BLOCK, LANES)),
                  _resident((1, D)),
                  _resident((D, D))],
        out_specs=rows3,
        out_shape=jax.ShapeDtypeStruct((BATCH, SEQ, D), F32),
        scratch_shapes=[pltpu.VMEM((ROWS_MIX, HEADS, HEAD_DIM, HEAD_DIM), F32),
                        pltpu.VMEM((ROWS_MIX, HEADS, HEAD_DIM, LANES), F32),
                        pltpu.VMEM((ROWS_MIX, HEADS, SUBLANES, LANES), F32),
                        pltpu.VMEM((ROWS_MIX * L_MIX, D), BF16)],
        compiler_params=pltpu.CompilerParams(dimension_semantics=("arbitrary", "arbitrary"),
                                             vmem_limit_bytes=VMEM_LIMIT),
        name="mixer",
    )(x3, g1, seg4(ua), seg4(v), seg4(q), kt, kt, seg4(mv), seg4(og),
      gates.reshape(BATCH, SEQ, LANES), ws, bs_rep, hn_g, w_out)


def _ffn_kernel(h_ref, sh_ref, sc_ref, g2_ref, ng_ref, w1_ref, w2_ref, fg_ref, o_ref):
    h = h_ref[...]
    xn = _rms_mod(h, ng_ref[...], sh_ref[...], sc_ref[...]).astype(BF16)
    acc = jnp.zeros((TM_FFN, D), F32)
    for c in range(FFN // FFN_CHUNK):
        cs = slice(c * FFN_CHUNK, (c + 1) * FFN_CHUNK)
        a = jnp.maximum(jnp.dot(xn, w1_ref[:, cs], preferred_element_type=F32), 0.0)
        acc = acc + jnp.dot((a * a).astype(BF16), w2_ref[cs, :], preferred_element_type=F32)
    h2 = h + g2_ref[...] * acc
    y = h2 * lax.rsqrt(jnp.mean(h2 * h2, axis=-1, keepdims=True) + EPS)
    o_ref[...] = y * fg_ref[...]


def _ffn(h1, sh2, sc2, g2, norm_g, w1, w2, final_g):
    nsb = SEQ // TM_FFN
    row = lambda i: (i, 0)
    per_batch = lambda i: (i // nsb, 0, 0)
    return pl.pallas_call(
        _ffn_kernel,
        grid=(TOKENS // TM_FFN,),
        in_specs=[pl.BlockSpec((TM_FFN, D), row),
                  pl.BlockSpec((None, 1, D), per_batch),
                  pl.BlockSpec((None, 1, D), per_batch),
                  pl.BlockSpec((None, 1, D), per_batch),
                  _resident((1, D)),
                  _resident((D, FFN)),
                  _resident((FFN, D)),
                  _resident((1, D))],
        out_specs=pl.BlockSpec((TM_FFN, D), row),
        out_shape=jax.ShapeDtypeStruct((TOKENS, D), F32),
        compiler_params=pltpu.CompilerParams(dimension_semantics=("arbitrary",),
                                             vmem_limit_bytes=VMEM_LIMIT),
        name="ffn",
    )(h1, sh2, sc2, g2, norm_g, w1, w2, final_g)


def kernel(x, c, w_ada, b_ada, norm1_g, w_in, conv_w, conv_b, mlstm_gate_b, gmlp_ln_g, gmlp_ln_b,
           gmlp_ws, gmlp_bs, mlstm_hn_g, w_out, norm2_g, w_ff1, w_ff2, final_g):
    l = 0
    mod = _ada(c, w_ada[l], b_ada[l]).reshape(BATCH, 6, 1, D)
    sh1, sc1, g1, sh2, sc2, g2 = (mod[:, t] for t in range(6))

    wt = jnp.swapaxes(w_in, 1, 2)[l]
    w_all = _wprep(wt)
    w_gate = wt[OFF_I:OFF_GA]
    gate_bias = jnp.pad(mlstm_gate_b[l].reshape(1, 2 * HEADS), ((0, 0), (0, LANES - 2 * HEADS)))
    cw = conv_w[l].reshape(CONV_K, 2, NCH, CW).transpose(1, 2, 0, 3)
    cb = conv_b[l].reshape(2, NCH, 1, CW)

    x2 = x.reshape(TOKENS, D)
    ua, v, q, kt, mv, og, gates = _inproj(
        x2, sh1, sc1, norm1_g[l].reshape(1, D), w_all, w_gate, gate_bias, cw, cb,
        gmlp_ln_g[l].reshape(NCH, 1, CW), gmlp_ln_b[l].reshape(NCH, 1, CW))

    bs_rep = jnp.broadcast_to(gmlp_bs[l][:, :, None], (GMLP_GROUPS, GMLP_BLOCK, LANES))
    h1 = _mixer(x, g1, ua, v, q, kt, mv, og, gates,
                gmlp_ws[l], bs_rep, mlstm_hn_g[l].reshape(1, D), w_out[l].astype(BF16))
    h1 = h1.reshape(TOKENS, D)

    out = _ffn(h1, sh2, sc2, g2, norm2_g[l].reshape(1, D),
               w_ff1[l].astype(BF16), w_ff2[l].astype(BF16), final_g.reshape(1, D))
    return out.reshape(BATCH, SEQ, D)
```

```python
import jax
import jax.numpy as jnp
from jax import lax
from jax.experimental import pallas as pl
from jax.experimental.pallas import tpu as pltpu

D = 1024
BATCH = 16
SEQ = 2048
TOKENS = BATCH * SEQ
CHUNK = 64
GMLP_BLOCK = 128
GMLP_GROUPS = 8
GROUP_DIM = D // GMLP_GROUPS
HEADS = 4
HEAD_DIM = 256
CONV_K = 4
FFN = 4 * D
EPS = 1e-6
LANES = 128
SUBLANES = 8

OFF_I = 6 * D
OFF_GA = OFF_I + 2 * HEADS
N_SEG = 8
SEG_U, SEG_V, SEG_Q, SEG_K, SEG_MV, SEG_O, SEG_GA, SEG_GB = range(N_SEG)
CW = 256
NCH = D // CW

TM_IN = 512
L_MIX = 256
ROWS_MIX = 2
TM_FFN = 1024
FFN_CHUNK = 1024
ADA_COLS = 1536

F32 = jnp.float32
BF16 = jnp.bfloat16
VMEM_LIMIT = 56 * 1024 * 1024


def _resident(shape):
    nd = len(shape)
    return pl.BlockSpec(shape, lambda *_: (0,) * nd, pipeline_mode=pl.Buffered(1))


def _hi_lo(a):
    hi = a.astype(BF16)
    return hi, (a - hi.astype(F32)).astype(BF16)


def _ada_kernel(c_ref, w_ref, b_ref, o_ref):
    c = c_ref[...]
    ca_hi, ca_lo = _hi_lo(c * jax.nn.sigmoid(c))
    w_hi, w_lo = _hi_lo(w_ref[...])
    p = jnp.dot(jnp.concatenate([ca_hi, ca_lo], axis=0), w_hi, preferred_element_type=F32)
    o_ref[...] = (p[:BATCH] + p[BATCH:]
                  + jnp.dot(ca_hi, w_lo, preferred_element_type=F32) + b_ref[...])


def _ada(c, w, b):
    n = w.shape[1]
    return pl.pallas_call(
        _ada_kernel,
        grid=(n // ADA_COLS,),
        in_specs=[pl.BlockSpec((BATCH, D), lambda j: (0, 0)),
                  pl.BlockSpec((D, ADA_COLS), lambda j: (0, j)),
                  pl.BlockSpec((1, ADA_COLS), lambda j: (0, j))],
        out_specs=pl.BlockSpec((BATCH, ADA_COLS), lambda j: (0, j)),
        out_shape=jax.ShapeDtypeStruct((BATCH, n), F32),
        compiler_params=pltpu.CompilerParams(vmem_limit_bytes=VMEM_LIMIT),
        name="ada",
    )(c, w, b.reshape(1, n))


def _wprep_kernel(a_ref, b_ref, o_ref):
    s = pl.program_id(0)

    def emit(rows):
        for c in range(NCH):
            o_ref[c] = rows(c * CW, (c + 1) * CW).T.astype(BF16)

    @pl.when(s < SEG_GA)
    def _():
        emit(lambda lo, hi: a_ref[lo:hi, :])

    @pl.when(s >= SEG_GA)
    def _():
        skip = 2 * HEADS
        emit(lambda lo, hi: a_ref[lo + skip:hi + skip, :] if hi < D else
             jnp.concatenate([a_ref[lo + skip:, :], b_ref[...]], axis=0))


def _wprep(wt):
    gate_rows = 2 * HEADS
    return pl.pallas_call(
        _wprep_kernel,
        grid=(N_SEG,),
        in_specs=[pl.BlockSpec((D, D), lambda s: (s, 0)),
                  pl.BlockSpec((gate_rows, D), lambda s: ((s + 1) * (D // gate_rows), 0))],
        out_specs=pl.BlockSpec((None, NCH, D, CW), lambda s: (s, 0, 0, 0)),
        out_shape=jax.ShapeDtypeStruct((N_SEG, NCH, D, CW), BF16),
        compiler_params=pltpu.CompilerParams(vmem_limit_bytes=VMEM_LIMIT),
        name="wprep",
    )(wt, wt)


def _rms_mod(x, g, shift, scale):
    y = x * lax.rsqrt(jnp.mean(x * x, axis=-1, keepdims=True) + EPS)
    return y * (g * (1.0 + scale)) + shift


def _conv_silu(z, halo_ref, c, w, b):
    assert CONV_K == 4 and w.shape[0] == CONV_K
    tm = z.shape[0]
    zz = jnp.concatenate([halo_ref[c], z], axis=0)
    z1 = pltpu.roll(zz, 1, axis=0)
    older = pltpu.roll(w[1:2, :] * zz + w[0:1, :] * z1, 2, axis=0)
    acc = (b + w[3:4, :] * z + w[2:3, :] * z1[SUBLANES:, :]) + older[SUBLANES:, :]
    halo_ref[c] = z[tm - SUBLANES:, :]
    return acc * jax.nn.sigmoid(acc)


def _inproj_kernel(x_ref, sh_ref, sc_ref, g_ref, w_ref, wg_ref, gb_ref, cw_ref, cb_ref,
                   lng_ref, lnb_ref,
                   ua_ref, v_ref, q_ref, kt_ref, mv_ref, og_ref, gate_ref,
                   xn_scr, gv_scr, s1_scr, halo_scr):
    i = pl.program_id(0)

    @pl.when((i % (SEQ // TM_IN)) == 0)
    def _():
        halo_scr[...] = jnp.zeros(halo_scr.shape, F32)

    xn_scr[...] = _rms_mod(x_ref[...], g_ref[...], sh_ref[...], sc_ref[...]).astype(BF16)
    s1_scr[...] = jnp.zeros(s1_scr.shape, F32)

    def seg(k, c):
        return jnp.dot(xn_scr[...], w_ref[k, c], preferred_element_type=F32)

    for c in range(NCH):
        k = _conv_silu(seg(SEG_K, c), halo_scr.at[1], c, cw_ref[1, c], cb_ref[1, c])
        kt_ref[c] = k.T.astype(BF16)
        sig_o = jax.nn.sigmoid(seg(SEG_O, c))
        q = _conv_silu(seg(SEG_Q, c), halo_scr.at[0], c, cw_ref[0, c], cb_ref[0, c])
        q_ref[c] = (q * (HEAD_DIM ** -0.5)).astype(BF16)
        og_ref[c] = (jax.nn.sigmoid(seg(SEG_GB, c)) * sig_o).astype(BF16)
        gv = jax.nn.gelu(seg(SEG_V, c))
        gv_scr[c] = gv
        s1_scr[...] += gv[:, :LANES] + gv[:, LANES:]
        sig_ga = jax.nn.sigmoid(seg(SEG_GA, c))
        ua_ref[c] = (jax.nn.gelu(seg(SEG_U, c)) * sig_ga).astype(BF16)
        mv_ref[c] = seg(SEG_MV, c).astype(BF16)

    wg = jnp.concatenate([wg_ref[...], jnp.zeros((LANES - 2 * HEADS, D), F32)], axis=0)
    gp = lax.dot_general(xn_scr[...], wg.astype(BF16), (((1,), (1,)), ((), ())),
                         preferred_element_type=F32) + gb_ref[...]
    lane = lax.broadcasted_iota(jnp.int32, gp.shape, 1)
    gate_ref[...] = jnp.where(lane < HEADS, gp, jax.nn.log_sigmoid(gp))

    mu = jnp.sum(s1_scr[...], axis=-1, keepdims=True) * (1.0 / D)
    ssq = jnp.zeros((TM_IN, LANES), F32)
    for c in range(NCH):
        dv = gv_scr[c] - mu
        gv_scr[c] = dv
        dv = dv * dv
        ssq = ssq + (dv[:, :LANES] + dv[:, LANES:])
    rstd = lax.rsqrt(jnp.sum(ssq, axis=-1, keepdims=True) * (1.0 / D) + EPS)
    for c in range(NCH):
        v_ref[c] = ((gv_scr[c] * rstd) * lng_ref[c] + lnb_ref[c]).astype(BF16)


def _inproj(x2, sh1, sc1, norm_g, w_all, w_gate, gate_bias, conv_w, conv_b, ln_g, ln_b):
    nsb = SEQ // TM_IN
    row = lambda i: (i, 0)
    per_batch = lambda i: (i // nsb, 0, 0)
    seg_out = pl.BlockSpec((NCH, TM_IN, CW), lambda i: (0, i, 0))
    seg_shape = jax.ShapeDtypeStruct((NCH, TOKENS, CW), BF16)
    seg_outs = [seg_out, seg_out, seg_out,
                pl.BlockSpec((NCH, CW, TM_IN), lambda i: (0, 0, i)), seg_out, seg_out]
    seg_shapes = [seg_shape, seg_shape, seg_shape,
                  jax.ShapeDtypeStruct((NCH, CW, TOKENS), BF16), seg_shape, seg_shape]
    return pl.pallas_call(
        _inproj_kernel,
        grid=(TOKENS // TM_IN,),
        in_specs=[pl.BlockSpec((TM_IN, D), row),
                  pl.BlockSpec((None, 1, D), per_batch),
                  pl.BlockSpec((None, 1, D), per_batch),
                  _resident((1, D)),
                  _resident((N_SEG, NCH, D, CW)),
                  _resident((2 * HEADS, D)),
                  _resident((1, LANES)),
                  _resident((2, NCH, CONV_K, CW)),
                  _resident((2, NCH, 1, CW)),
                  _resident((NCH, 1, CW)),
                  _resident((NCH, 1, CW))],
        out_specs=seg_outs + [pl.BlockSpec((TM_IN, LANES), row)],
        out_shape=seg_shapes + [jax.ShapeDtypeStruct((TOKENS, LANES), F32)],
        scratch_shapes=[pltpu.VMEM((TM_IN, D), BF16),
                        pltpu.VMEM((NCH, TM_IN, CW), F32),
                        pltpu.VMEM((TM_IN, LANES), F32),
                        pltpu.VMEM((2, NCH, SUBLANES, CW), F32)],
        compiler_params=pltpu.CompilerParams(dimension_semantics=("arbitrary",),
                                             vmem_limit_bytes=VMEM_LIMIT),
        name="inproj",
    )(x2, sh1, sc1, norm_g, w_all, w_gate, gate_bias, conv_w, conv_b, ln_g, ln_b)


def _cumsum_rows(x):
    n = x.shape[0]
    row = lax.broadcasted_iota(jnp.int32, x.shape, 0)
    k = 1
    while k < n:
        x = x + jnp.where(row >= k, pltpu.roll(x, k, axis=0), 0.0)
        k *= 2
    return x


def _rep2(a):
    return jnp.concatenate([a, a], axis=1)


def _mixer_kernel(x_ref, g1_ref, ua_ref, v_ref, q_ref, kt0_ref, kt1_ref, mv_ref, og_ref,
                  gate_ref, ws_ref, bsb_ref, hng_ref, wout_ref,
                  h1_ref,
                  c_scr, n_scr, m_scr, merged_scr):
    L = L_MIX
    j = pl.program_id(1)

    @pl.when(j == 0)
    def _():
        c_scr[...] = jnp.zeros(c_scr.shape, F32)
        n_scr[...] = jnp.zeros(n_scr.shape, F32)
        m_scr[...] = jnp.zeros(m_scr.shape, F32)

    kt_refs = (kt0_ref, kt1_ref)
    scans = []
    for r in range(ROWS_MIX):
        gates = gate_ref[r]
        bcum = _cumsum_rows(gates)
        scans.append((gates.T, bcum, bcum.T))
    row = lax.broadcasted_iota(jnp.int32, (L, L), 0)
    col = lax.broadcasted_iota(jnp.int32, (L, L), 1)
    causal = col <= row
    ones_rhs = jnp.ones((L, LANES), BF16)

    brow = lax.broadcasted_iota(jnp.int32, (GMLP_BLOCK, GMLP_BLOCK), 0) // CHUNK
    bcol = lax.broadcasted_iota(jnp.int32, (GMLP_BLOCK, GMLP_BLOCK), 1) // CHUNK
    wmask = bcol <= brow
    nblk = L // GMLP_BLOCK

    for h in range(HEADS):
        hs = slice(h * HEAD_DIM, (h + 1) * HEAD_DIM)
        for r in range(ROWS_MIX):
            gates_t, bcum, bcum_t = scans[r]
            li_r = gates_t[h:h + 1, :]
            b_r = bcum_t[HEADS + h:HEADS + h + 1, :]
            b_c = jnp.broadcast_to(bcum[:, HEADS + h:HEADS + h + 1], (L, LANES))
            m_prev = m_scr[r, h, 0:1, 0:1]
            dmat = jnp.where(causal, _rep2(b_c) + (li_r - b_r), -jnp.inf)
            m_inter = b_c + m_prev
            m_t = jnp.maximum(m_inter, jnp.max(dmat, axis=-1, keepdims=True))
            wts = jnp.exp(dmat - _rep2(m_t))
            qh = q_ref[h, r]
            kt = kt_refs[r][h]
            vh = mv_ref[h, r]
            s = jnp.dot(qh, kt, preferred_element_type=F32) * wts
            sb = s.astype(BF16)
            inter = jnp.exp(m_inter - m_t)
            c_old = c_scr[r, h]
            n_old = n_scr[r, h]
            n_hi = n_old.astype(BF16)
            n_lo = (n_old - n_hi.astype(F32)).astype(BF16)
            qn2 = jnp.dot(qh, jnp.concatenate([n_hi, n_lo], axis=1), preferred_element_type=F32)
            qn = qn2[:, :LANES] + qn2[:, LANES:]
            num = (jnp.dot(sb, vh, preferred_element_type=F32)
                   + _rep2(inter) * jnp.dot(qh, c_old.astype(BF16), preferred_element_type=F32))
            den = jnp.dot(sb, ones_rhs, preferred_element_type=F32) + inter * qn
            inv = 1.0 / jnp.maximum(jnp.abs(den), jnp.exp(-m_t))
            hh = num * _rep2(inv)

            b_last = b_r[:, L - 1:L]
            g_r = (b_last - b_r) + li_r
            m_new = jnp.maximum(b_last + m_prev, jnp.max(g_r, axis=-1, keepdims=True))
            decay = jnp.exp((b_last + m_prev) - m_new)
            wkt = jnp.exp(g_r - m_new).astype(BF16) * kt
            c_scr[r, h] = decay * c_old + jnp.dot(wkt, vh, preferred_element_type=F32)
            n_scr[r, h] = decay * n_old + jnp.dot(wkt, ones_rhs, preferred_element_type=F32)
            m_scr[r, h] = jnp.broadcast_to(m_new, (SUBLANES, LANES))

            mu = jnp.mean(hh, axis=-1, keepdims=True)
            dh = hh - mu
            var = jnp.mean(dh * dh, axis=-1, keepdims=True)
            y_n = (dh * lax.rsqrt(var + EPS)) * hng_ref[:, hs]

            for half in range(CW // GROUP_DIM):
                g = h * (CW // GROUP_DIM) + half
                ls = slice(half * GROUP_DIM, (half + 1) * GROUP_DIM)
                wsg = jnp.where(wmask, ws_ref[g], 0.0).astype(BF16)
                vcat = jnp.concatenate(
                    [v_ref[h, r, n * GMLP_BLOCK:(n + 1) * GMLP_BLOCK, ls] for n in range(nblk)],
                    axis=1)
                mixed = jnp.dot(wsg, vcat, preferred_element_type=F32) + _rep2(bsb_ref[g])
                for n in range(nblk):
                    rs = slice(n * GMLP_BLOCK, (n + 1) * GMLP_BLOCK)
                    gate_a = ua_ref[h, r, rs, ls].astype(F32)
                    gate_b = og_ref[h, r, rs, ls].astype(F32)
                    merged = (gate_a * mixed[:, n * GMLP_BLOCK:(n + 1) * GMLP_BLOCK]
                              + gate_b * y_n[rs, ls])
                    merged_scr[r * L + n * GMLP_BLOCK:r * L + (n + 1) * GMLP_BLOCK,
                               g * GROUP_DIM:(g + 1) * GROUP_DIM] = merged.astype(BF16)

    proj = jnp.dot(merged_scr[...], wout_ref[...], preferred_element_type=F32)
    for r in range(ROWS_MIX):
        h1_ref[r] = x_ref[r] + g1_ref[r] * proj[r * L:(r + 1) * L, :]


def _mixer(x3, g1, ua, v, q, kt, mv, og, gates, ws, bs_rep, hn_g, w_out):
    nsb = SEQ // L_MIX
    rows3 = pl.BlockSpec((ROWS_MIX, L_MIX, D), lambda b, j: (b, j, 0))
    seg_in = pl.BlockSpec((NCH, ROWS_MIX, L_MIX, CW), lambda b, j: (0, b, j, 0))
    kt_in = [pl.BlockSpec((NCH, CW, L_MIX), lambda b, j, r=r: (0, 0, (ROWS_MIX * b + r) * nsb + j))
             for r in range(ROWS_MIX)]
    seg4 = lambda a: a.reshape(NCH, BATCH, SEQ, CW)
    return pl.pallas_call(
        _mixer_kernel,
        grid=(BATCH // ROWS_MIX, nsb),
        in_specs=[rows3,
                  pl.BlockSpec((ROWS_MIX, 1, D), lambda b, j: (b, 0, 0)),
                  seg_in, seg_in, seg_in, kt_in[0], kt_in[1], seg_in, seg_in,
                  pl.BlockSpec((ROWS_MIX, L_MIX, LANES), lambda b, j: (b, j, 0)),
                  _resident((GMLP_GROUPS, GMLP_BLOCK, GMLP_BLOCK)),
                  _resident((GMLP_GROUPS, GMLP_BLOCK, LANES)),
                  _resident((1, D)),
                  _resident((D, D))],
        out_specs=rows3,
        out_shape=jax.ShapeDtypeStruct((BATCH, SEQ, D), F32),
        scratch_shapes=[pltpu.VMEM((ROWS_MIX, HEADS, HEAD_DIM, HEAD_DIM), F32),
                        pltpu.VMEM((ROWS_MIX, HEADS, HEAD_DIM, LANES), F32),
                        pltpu.VMEM((ROWS_MIX, HEADS, SUBLANES, LANES), F32),
                        pltpu.VMEM((ROWS_MIX * L_MIX, D), BF16)],
        compiler_params=pltpu.CompilerParams(dimension_semantics=("arbitrary", "arbitrary"),
                                             vmem_limit_bytes=VMEM_LIMIT),
        name="mixer",
    )(x3, g1, seg4(ua), seg4(v), seg4(q), kt, kt, seg4(mv), seg4(og),
      gates.reshape(BATCH, SEQ, LANES), ws, bs_rep, hn_g, w_out)


def _ffn_kernel(h_ref, sh_ref, sc_ref, g2_ref, ng_ref, w1_ref, w2_ref, fg_ref, o_ref):
    h = h_ref[...]
    xn = _rms_mod(h, ng_ref[...], sh_ref[...], sc_ref[...]).astype(BF16)
    acc = jnp.zeros((TM_FFN, D), F32)
    for c in range(FFN // FFN_CHUNK):
        cs = slice(c * FFN_CHUNK, (c + 1) * FFN_CHUNK)
        a = jnp.maximum(jnp.dot(xn, w1_ref[:, cs], preferred_element_type=F32), 0.0)
        acc = acc + jnp.dot((a * a).astype(BF16), w2_ref[cs, :], preferred_element_type=F32)
    h2 = h + g2_ref[...] * acc
    y = h2 * lax.rsqrt(jnp.mean(h2 * h2, axis=-1, keepdims=True) + EPS)
    o_ref[...] = y * fg_ref[...]


def _ffn(h1, sh2, sc2, g2, norm_g, w1, w2, final_g):
    nsb = SEQ // TM_FFN
    row = lambda i: (i, 0)
    per_batch = lambda i: (i // nsb, 0, 0)
    return pl.pallas_call(
        _ffn_kernel,
        grid=(TOKENS // TM_FFN,),
        in_specs=[pl.BlockSpec((TM_FFN, D), row),
                  pl.BlockSpec((None, 1, D), per_batch),
                  pl.BlockSpec((None, 1, D), per_batch),
                  pl.BlockSpec((None, 1, D), per_batch),
                  _resident((1, D)),
                  _resident((D, FFN)),
                  _resident((FFN, D)),
                  _resident((1, D))],
        out_specs=pl.BlockSpec((TM_FFN, D), row),
        out_shape=jax.ShapeDtypeStruct((TOKENS, D), F32),
        compiler_params=pltpu.CompilerParams(dimension_semantics=("arbitrary",),
                                             vmem_limit_bytes=VMEM_LIMIT),
        name="ffn",
    )(h1, sh2, sc2, g2, norm_g, w1, w2, final_g)


def kernel(x, c, w_ada, b_ada, norm1_g, w_in, conv_w, conv_b, mlstm_gate_b, gmlp_ln_g, gmlp_ln_b,
           gmlp_ws, gmlp_bs, mlstm_hn_g, w_out, norm2_g, w_ff1, w_ff2, final_g):
    l = 0
    mod = _ada(c, w_ada[l], b_ada[l]).reshape(BATCH, 6, 1, D)
    sh1, sc1, g1, sh2, sc2, g2 = (mod[:, t] for t in range(6))

    wt = jnp.swapaxes(w_in, 1, 2)[l]
    w_all = _wprep(wt)
    w_gate = wt[OFF_I:OFF_GA]
    gate_bias = jnp.pad(mlstm_gate_b[l].reshape(1, 2 * HEADS), ((0, 0), (0, LANES - 2 * HEADS)))
    cw = conv_w[l].reshape(CONV_K, 2, NCH, CW).transpose(1, 2, 0, 3)
    cb = conv_b[l].reshape(2, NCH, 1, CW)

    x2 = x.reshape(TOKENS, D)
    ua, v, q, kt, mv, og, gates = _inproj(
        x2, sh1, sc1, norm1_g[l].reshape(1, D), w_all, w_gate, gate_bias, cw, cb,
        gmlp_ln_g[l].reshape(NCH, 1, CW), gmlp_ln_b[l].reshape(NCH, 1, CW))

    bs_rep = jnp.broadcast_to(gmlp_bs[l][:, :, None], (GMLP_GROUPS, GMLP_BLOCK, LANES))
    h1 = _mixer(x, g1, ua, v, q, kt, mv, og, gates,
                gmlp_ws[l], bs_rep, mlstm_hn_g[l].reshape(1, D), w_out[l].astype(BF16))
    h1 = h1.reshape(TOKENS, D)

    out = _ffn(h1, sh2, sc2, g2, norm2_g[l].reshape(1, D),
               w_ff1[l].astype(BF16), w_ff2[l].astype(BF16), final_g.reshape(1, D))
    return out.reshape(BATCH, SEQ, D)
```
